```python
import math
import jax, jax.numpy as jnp
from jax import lax
import numpy as np

D_MODEL = 1024
BATCH = 1
SEQ = 16384
DEPTH = 4
DEC_BATCH = 32
DEC_SEQ = 64
PAST_LEN = 1024

CHUNK = 64
N_META = 16
ROPE_THETA = 10000.0
LN_EPS = 1e-5
NEG_INF = -1e30
DEEPNORM_ALPHA = (2 * DEPTH) ** 0.25
DEEPNORM_BETA = (8 * DEPTH) ** -0.25

A_HEADS = 8
A_HEAD_DIM = 64
A_WIDTH = A_HEADS * 2 * A_HEAD_DIM
B_HEADS = 8
B_KV_HEADS = 2
B_HEAD_DIM = 128
B_WIDTH = B_HEADS * B_HEAD_DIM
IDX_HEADS = 4
IDX_DIM = 64
TOPK_MAX = 256
N_EXPERTS = 64
TOP_K = 8
N_GROUPS = 8
TOPK_GROUPS = 4
EXPERT_DIM = 256
SHARED_DIM = 256
ROUTE_SCALE = 2.5

Q_BLOCK = 128
ROW_BLOCK = 256
BIG_CHUNK_ID = 1 << 30

IN_WIDTHS = (2 * A_HEADS * A_HEAD_DIM, 2 * A_HEADS * A_HEAD_DIM, A_WIDTH,
             B_WIDTH, B_KV_HEADS * B_HEAD_DIM, B_KV_HEADS * B_HEAD_DIM,
             IDX_HEADS * IDX_DIM, IDX_DIM, IDX_HEADS, D_MODEL, D_MODEL)
IN_SPLITS = tuple(int(v) for v in np.cumsum(IN_WIDTHS)[:-1])
N_IN = sum(IN_WIDTHS)

kernel_name = 'hybrid_diffattn_dsa_moe_stream_step'


def _layernorm(x, g, b):
    xf = x.astype(jnp.float32)
    mu = jnp.mean(xf, -1, keepdims=True)
    var = jnp.mean(jnp.square(xf - mu), -1, keepdims=True)
    y = (xf - mu) * lax.rsqrt(var + LN_EPS) * g.astype(jnp.float32) + b.astype(jnp.float32)
    return y.astype(x.dtype)


def _rmsnorm(x, g):
    xf = x.astype(jnp.float32)
    y = xf * lax.rsqrt(jnp.mean(jnp.square(xf), -1, keepdims=True) + LN_EPS) * g.astype(jnp.float32)
    return y.astype(x.dtype)


def _rope(x, pos):
    half = x.shape[-1] // 2
    inv = ROPE_THETA ** (-jnp.arange(half, dtype=jnp.float32) / half)
    ang = pos.astype(jnp.float32)[:, None] * inv[None, :]
    cos = jnp.cos(ang)[:, None, :]
    sin = jnp.sin(ang)[:, None, :]
    xf = x.astype(jnp.float32)
    x1, x2 = xf[..., :half], xf[..., half:]
    return jnp.concatenate([x1 * cos - x2 * sin, x2 * cos + x1 * sin], -1).astype(x.dtype)


def _project_in(h, pos, w_in_l):
    B, S, _ = h.shape
    qa, ka, va, qb, kb, vb, qi, ki, wi, ga, gb = jnp.split(h @ w_in_l, IN_SPLITS, axis=-1)
    qa = _rope(qa.reshape(B, S, 2 * A_HEADS, A_HEAD_DIM), pos)
    ka = _rope(ka.reshape(B, S, 2 * A_HEADS, A_HEAD_DIM), pos)
    va = va.reshape(B, S, A_HEADS, 2 * A_HEAD_DIM)
    qb = _rope(qb.reshape(B, S, B_HEADS, B_HEAD_DIM), pos)
    kb = _rope(kb.reshape(B, S, B_KV_HEADS, B_HEAD_DIM), pos)
    vb = vb.reshape(B, S, B_KV_HEADS, B_HEAD_DIM)
    qi = _rope(qi.reshape(B, S, IDX_HEADS, IDX_DIM), pos)
    ki = _rope(ki.reshape(B, S, 1, IDX_DIM), pos)[:, :, 0]
    wi = wi * (IDX_HEADS ** -0.5)
    return (qa, qb, qi, wi), (ka, va, kb, vb, ki), (ga, gb)


def _diff_lambda(l, lq1, lk1, lq2, lk2):
    lam_init = 0.8 - 0.6 * math.exp(-0.3 * l)
    lam = (jnp.exp(jnp.sum(lq1.astype(jnp.float32) * lk1.astype(jnp.float32)))
           - jnp.exp(jnp.sum(lq2.astype(jnp.float32) * lk2.astype(jnp.float32))) + lam_init)
    return lam, lam_init


def _diff_attn(q, k, v, mask, lam, lam_init, subln_g):
    B, Q = q.shape[:2]
    s = jnp.einsum('bqhd,bshd->bhqs', q, k).astype(jnp.float32) * (A_HEAD_DIM ** -0.5)
    p = jax.nn.softmax(jnp.where(mask[:, None], s, NEG_INF), axis=-1)
    p = p.reshape(B, A_HEADS, 2, Q, p.shape[-1])
    d = (p[:, :, 0] - lam * p[:, :, 1]).astype(v.dtype)
    o = jnp.einsum('bhqs,bshe->bqhe', d, v)
    o = _rmsnorm(o, subln_g) * (1.0 - lam_init)
    return o.reshape(B, Q, A_WIDTH)


def _dsa_attn(q, qi, wi, k, v, ki, mask, n_sel):
    B, Q = q.shape[:2]
    rel = jax.nn.relu(jnp.einsum('bqhd,bsd->bqhs', qi, ki).astype(jnp.float32) * (IDX_DIM ** -0.5))
    score = jnp.einsum('bqh,bqhs->bqs', wi.astype(jnp.float32), rel)
    score = jnp.where(mask, score, NEG_INF)
    top_val, top_idx = lax.top_k(score, n_sel)
    sel_ok = top_val > 0.5 * NEG_INF
    kg = jax.vmap(lambda kk, ii: kk[ii])(k, top_idx)
    vg = jax.vmap(lambda vv, ii: vv[ii])(v, top_idx)
    qg = q.reshape(B, Q, B_KV_HEADS, B_HEADS // B_KV_HEADS, B_HEAD_DIM)
    s = jnp.einsum('bqcgd,bqncd->bcgqn', qg, kg).astype(jnp.float32) * (B_HEAD_DIM ** -0.5)
    p = jax.nn.softmax(jnp.where(sel_ok[:, None, None], s, NEG_INF), axis=-1)
    o = jnp.einsum('bcgqn,bqncd->bqcgd', p.astype(vg.dtype), vg)
    return o.reshape(B, Q, B_WIDTH)


def _merge(ya, yb, ga, gb, w_pa, w_pb, w_o):
    return (jax.nn.sigmoid(ga) * (ya @ w_pa) + jax.nn.sigmoid(gb) * (yb @ w_pb)) @ w_o


def _mixer_prompt(h, pos, cid, n_sel, w_in_l, lam, lam_init, subln_l, w_pa, w_pb, w_o):
    B, L, _ = h.shape
    (qa, qb, qi, wi), rows, (ga, gb) = _project_in(h, pos, w_in_l)
    ka, va, kb, vb, ki = rows
    nb = -(-L // Q_BLOCK)
    pad = nb * Q_BLOCK - L

    def to_blocks(t):
        t = jnp.pad(t, [(0, 0), (0, pad)] + [(0, 0)] * (t.ndim - 2))
        return jnp.moveaxis(t.reshape((B, nb, Q_BLOCK) + t.shape[2:]), 1, 0)

    def from_blocks(t):
        return jnp.moveaxis(t, 0, 1).reshape(B, nb * Q_BLOCK, t.shape[-1])[:, :L]

    cid_blocks = jnp.pad(cid, (0, pad), constant_values=BIG_CHUNK_ID).reshape(nb, Q_BLOCK)

    def one_block(xs):
        qa_b, qb_b, qi_b, wi_b, cid_b = xs
        mask = (cid[None, :] <= cid_b[:, None])[None]
        return (_diff_attn(qa_b, ka, va, mask, lam, lam_init, subln_l),
                _dsa_attn(qb_b, qi_b, wi_b, kb, vb, ki, mask, n_sel))

    ya, yb = lax.map(one_block, (to_blocks(qa), to_blocks(qb), to_blocks(qi), to_blocks(wi), cid_blocks))
    y = _merge(from_blocks(ya), from_blocks(yb), ga, gb, w_pa, w_pb, w_o)
    return y, rows


def _mixer_sample(h, pos, past, n_sel, w_in_l, lam, lam_init, subln_l, w_pa, w_pb, w_o):
    (qa, qb, qi, wi), rows, (ga, gb) = _project_in(h, pos, w_in_l)
    ka, va, kb, vb, ki = (jnp.concatenate([p, r.astype(p.dtype)], axis=1) for p, r in zip(past, rows))
    mask = jnp.ones((1, h.shape[1], ka.shape[1]), dtype=bool)
    ya = _diff_attn(qa, ka, va, mask, lam, lam_init, subln_l)
    yb = _dsa_attn(qb, qi, wi, kb, vb, ki, mask, n_sel)
    return _merge(ya, yb, ga, gb, w_pa, w_pb, w_o), rows


def _map_rows(fn, x):
    T = x.shape[0]
    nb = -(-T // ROW_BLOCK)
    xp = jnp.pad(x, ((0, nb * ROW_BLOCK - T), (0, 0)))
    y = lax.map(fn, xp.reshape(nb, ROW_BLOCK, x.shape[1]))
    return y.reshape(nb * ROW_BLOCK, y.shape[-1])[:T]


def _moe(h, w_router, r_bias, we_g, we_u, we_d, ws_g, ws_u, ws_d):
    def block(xb):
        s = jax.nn.sigmoid((xb @ w_router).astype(jnp.float32))
        sel = s + r_bias.astype(jnp.float32)
        grp = sel.reshape(-1, N_GROUPS, N_EXPERTS // N_GROUPS)
        g_score = jnp.sum(lax.top_k(grp, 2)[0], axis=-1)
        _, g_idx = lax.top_k(g_score, TOPK_GROUPS)
        g_keep = jnp.sum(jax.nn.one_hot(g_idx, N_GROUPS, dtype=jnp.float32), axis=-2) > 0
        e_keep = jnp.repeat(g_keep, N_EXPERTS // N_GROUPS, axis=-1)
        _, e_idx = lax.top_k(jnp.where(e_keep, sel, NEG_INF), TOP_K)
        w = jnp.take_along_axis(s, e_idx, axis=-1)
        w = w / jnp.sum(w, -1, keepdims=True) * ROUTE_SCALE
        gates = jnp.einsum('tk,tke->te', w, jax.nn.one_hot(e_idx, N_EXPERTS, dtype=jnp.float32)).astype(xb.dtype)
        hg = jnp.einsum('td,edf->tef', xb, we_g)
        hu = jnp.einsum('td,edf->tef', xb, we_u)
        routed = jnp.einsum('tef,efd->td', jax.nn.silu(hg) * hu * gates[:, :, None], we_d)
        shared = (jax.nn.silu(xb @ ws_g) * (xb @ ws_u)) @ ws_d
        return routed + shared
    return _map_rows(block, h.reshape(-1, h.shape[-1])).reshape(h.shape)


def setup_inputs(seed: int = 0) -> dict:
    key = jax.random.key(seed)
    ks = jax.random.split(key, 32)

    def nrm(k, shape, scale):
        return jax.random.normal(k, shape, jnp.float32) * scale

    return {
        'x_prompt': nrm(ks[0], (BATCH, SEQ, D_MODEL), 1.0),
        'x_sample': nrm(ks[1], (DEC_BATCH, DEC_SEQ, D_MODEL), 1.0),
        'cache_a_k': nrm(ks[2], (DEPTH, DEC_BATCH, PAST_LEN, 2 * A_HEADS, A_HEAD_DIM), 1.0),
        'cache_a_v': nrm(ks[3], (DEPTH, DEC_BATCH, PAST_LEN, A_HEADS, 2 * A_HEAD_DIM), 1.0),
        'cache_b_k': nrm(ks[4], (DEPTH, DEC_BATCH, PAST_LEN, B_KV_HEADS, B_HEAD_DIM), 1.0),
        'cache_b_v': nrm(ks[5], (DEPTH, DEC_BATCH, PAST_LEN, B_KV_HEADS, B_HEAD_DIM), 1.0),
        'cache_b_idx': nrm(ks[6], (DEPTH, DEC_BATCH, PAST_LEN, IDX_DIM), 1.0),
        'meta_tokens': nrm(ks[7], (N_META, D_MODEL), 1.0),
        'ln_in_g': 1.0 + nrm(ks[8], (D_MODEL,), 0.02),
        'ln_in_b': nrm(ks[9], (D_MODEL,), 0.02),
        'w_in': nrm(ks[10], (DEPTH, D_MODEL, N_IN), D_MODEL ** -0.5),
        'lam_q1': nrm(ks[11], (DEPTH, A_HEAD_DIM), 0.1),
        'lam_k1': nrm(ks[12], (DEPTH, A_HEAD_DIM), 0.1),
        'lam_q2': nrm(ks[13], (DEPTH, A_HEAD_DIM), 0.1),
        'lam_k2': nrm(ks[14], (DEPTH, A_HEAD_DIM), 0.1),
        'subln_g': 1.0 + nrm(ks[15], (DEPTH, 2 * A_HEAD_DIM), 0.02),
        'w_proj_a': nrm(ks[16], (DEPTH, A_WIDTH, D_MODEL), A_WIDTH ** -0.5),
        'w_proj_b': nrm(ks[17], (DEPTH, B_WIDTH, D_MODEL), B_WIDTH ** -0.5),
        'w_out': nrm(ks[18], (DEPTH, D_MODEL, D_MODEL), D_MODEL ** -0.5 * DEEPNORM_BETA),
        'ln1_g': 1.0 + nrm(ks[19], (DEPTH, D_MODEL), 0.02),
        'ln1_b': nrm(ks[20], (DEPTH, D_MODEL), 0.02),
        'w_router': nrm(ks[21], (DEPTH, D_MODEL, N_EXPERTS), D_MODEL ** -0.5),
        'router_bias': nrm(ks[22], (DEPTH, N_EXPERTS), 0.01),
        'w_expert_gate': nrm(ks[23], (DEPTH, N_EXPERTS, D_MODEL, EXPERT_DIM), D_MODEL ** -0.5),
        'w_expert_up': nrm(ks[24], (DEPTH, N_EXPERTS, D_MODEL, EXPERT_DIM), D_MODEL ** -0.5),
        'w_expert_down': nrm(ks[25], (DEPTH, N_EXPERTS, EXPERT_DIM, D_MODEL), EXPERT_DIM ** -0.5 * DEEPNORM_BETA),
        'w_shared_gate': nrm(ks[26], (DEPTH, D_MODEL, SHARED_DIM), D_MODEL ** -0.5),
        'w_shared_up': nrm(ks[27], (DEPTH, D_MODEL, SHARED_DIM), D_MODEL ** -0.5),
        'w_shared_down': nrm(ks[28], (DEPTH, SHARED_DIM, D_MODEL), SHARED_DIM ** -0.5 * DEEPNORM_BETA),
        'ln2_g': 1.0 + nrm(ks[29], (DEPTH, D_MODEL), 0.02),
        'ln2_b': nrm(ks[30], (DEPTH, D_MODEL), 0.02),
    }


def reference(x_prompt, x_sample, cache_a_k, cache_a_v, cache_b_k, cache_b_v, cache_b_idx,
              meta_tokens, ln_in_g, ln_in_b, w_in, lam_q1, lam_k1, lam_q2, lam_k2, subln_g,
              w_proj_a, w_proj_b, w_out, ln1_g, ln1_b, w_router, router_bias,
              w_expert_gate, w_expert_up, w_expert_down, w_shared_gate, w_shared_up, w_shared_down,
              ln2_g, ln2_b):
    B, S, _ = x_prompt.shape
    past_len = cache_a_k.shape[2]
    dec_seq = x_sample.shape[1]
    n_sel_p = min(TOPK_MAX, S // 4)
    n_sel_s = min(TOPK_MAX, (past_len + dec_seq) // 4)

    pos_p = jnp.arange(N_META + S, dtype=jnp.int32)
    cid_p = jnp.where(pos_p < N_META, -1, (pos_p - N_META) // CHUNK)
    pos_s = past_len + jnp.arange(dec_seq, dtype=jnp.int32)

    meta = jnp.broadcast_to(meta_tokens[None].astype(x_prompt.dtype), (B, N_META, x_prompt.shape[-1]))
    hp = _layernorm(jnp.concatenate([meta, x_prompt], axis=1), ln_in_g, ln_in_b)
    hs = _layernorm(x_sample, ln_in_g, ln_in_b)

    rows_p = []
    rows_s = []
    for l in range(DEPTH):
        lam, lam_init = _diff_lambda(l, lam_q1[l], lam_k1[l], lam_q2[l], lam_k2[l])
        yp, rp = _mixer_prompt(hp, pos_p, cid_p, n_sel_p, w_in[l], lam, lam_init, subln_g[l],
                               w_proj_a[l], w_proj_b[l], w_out[l])
        past = (cache_a_k[l], cache_a_v[l], cache_b_k[l], cache_b_v[l], cache_b_idx[l])
        ys, rs = _mixer_sample(hs, pos_s, past, n_sel_s, w_in[l], lam, lam_init, subln_g[l],
                               w_proj_a[l], w_proj_b[l], w_out[l])
        rows_p.append(rp)
        rows_s.append(rs)
        hp = _layernorm(DEEPNORM_ALPHA * hp + yp, ln1_g[l], ln1_b[l])
        hs = _layernorm(DEEPNORM_ALPHA * hs + ys, ln1_g[l], ln1_b[l])
        moe_args = (w_router[l], router_bias[l], w_expert_gate[l], w_expert_up[l], w_expert_down[l],
                    w_shared_gate[l], w_shared_up[l], w_shared_down[l])
        hp = _layernorm(DEEPNORM_ALPHA * hp + _moe(hp, *moe_args), ln2_g[l], ln2_b[l])
        hs = _layernorm(DEEPNORM_ALPHA * hs + _moe(hs, *moe_args), ln2_g[l], ln2_b[l])

    y_prompt = hp[:, N_META:]
    y_sample = hs
    new_a_k_p = jnp.stack([r[0] for r in rows_p])
    new_a_v_p = jnp.stack([r[1] for r in rows_p])
    new_b_k_p = jnp.stack([r[2] for r in rows_p])
    new_b_v_p = jnp.stack([r[3] for r in rows_p])
    new_b_idx_p = jnp.stack([r[4] for r in rows_p])
    new_a_k_s = jnp.stack([r[0] for r in rows_s])
    new_a_v_s = jnp.stack([r[1] for r in rows_s])
    new_b_k_s = jnp.stack([r[2] for r in rows_s])
    new_b_v_s = jnp.stack([r[3] for r in rows_s])
    new_b_idx_s = jnp.stack([r[4] for r in rows_s])
    return (y_prompt, y_sample, new_a_k_p, new_a_v_p, new_b_k_p, new_b_v_p, new_b_idx_p,
            new_a_k_s, new_a_v_s, new_b_k_s, new_b_v_s, new_b_idx_s)
```

```python
import functools
import math

import numpy as np
import jax
import jax.numpy as jnp
from jax import lax
from jax.experimental import pallas as pl
from jax.experimental.pallas import tpu as pltpu

F32 = jnp.float32
BF16 = jnp.bfloat16
I32 = jnp.int32

MODEL_DEPTH = 4
CHUNK = 64
N_META = 16
ROPE_THETA = 10000.0
LN_EPS = 1e-5
NEG_INF = -1e30
DEEPNORM_ALPHA = (2 * MODEL_DEPTH) ** 0.25
A_HEADS = 8
A_HEAD_DIM = 64
B_HEADS = 8
B_KV_HEADS = 2
B_GROUP = B_HEADS // B_KV_HEADS
B_HEAD_DIM = 128
IDX_HEADS = 4
IDX_DIM = 64
TOPK_MAX = 256
N_EXPERTS = 64
TOP_K = 8
N_GROUPS = 8
GROUP_SIZE = N_EXPERTS // N_GROUPS
TOPK_GROUPS = 4
ROUTE_SCALE = 2.5

LANES = 128
D_MODEL = 1024
N_PROJ = 4992
COL_QA, COL_KA, COL_VA, COL_QB, COL_KB, COL_VB, COL_QI, COL_KIWI = (
    0, 1024, 2048, 3072, 4096, 4352, 4608, 4864)
N_RAW_PROJ = 4932

TILE_K = 512
FRONT_PAD = TILE_K - N_META
TILE_Q_A = 256
TILE_Q_B = 256
TILE_ROWS = 256
TILE_MOE = 512
EXPERT_BLOCK = 4
VMEM_LIMIT = 56 * 1024 * 1024
BIG_CHUNK = 1 << 30
INT_MIN = -(2 ** 31)


def _f32_key(v):
    b = int(np.float32(v).view(np.int32))
    return b ^ ((b >> 31) & 0x7FFFFFFF)


KEY_HALF_NEG = _f32_key(0.5 * NEG_INF)


def _dot(a, b):
    return jnp.dot(a, b, preferred_element_type=F32)


def _dot_nt(a, b):
    return lax.dot_general(a, b, (((1,), (1,)), ((), ())), preferred_element_type=F32)


def _layernorm(x, g, b):
    mu = jnp.mean(x, -1, keepdims=True)
    xc = x - mu
    var = jnp.mean(xc * xc, -1, keepdims=True)
    return xc * lax.rsqrt(var + LN_EPS) * g + b


def _sort_key(x):
    bits = lax.bitcast_convert_type(x, I32)
    return bits ^ ((bits >> 31) & 0x7FFFFFFF)


def _kth_largest_key(count_ge, rows, k):
    def body(b, tu):
        cand = tu | jnp.left_shift(jnp.int32(1), 31 - b)
        c = count_ge(cand ^ INT_MIN)
        return jnp.where(c >= k, cand, tu)
    tu = lax.fori_loop(0, 32, body, jnp.zeros((rows, 1), I32))
    return tu ^ INT_MIN


def _params(*sem):
    return pltpu.CompilerParams(dimension_semantics=sem, vmem_limit_bytes=VMEM_LIMIT)


def _ln_kernel(x_ref, g_ref, b_ref, o_ref):
    o_ref[...] = _layernorm(x_ref[...], g_ref[...], b_ref[...])


def _ln_rows(x, g, b):
    n, d = x.shape
    row = pl.BlockSpec((TILE_ROWS, d), lambda i: (i, 0))
    vec = pl.BlockSpec((1, d), lambda i: (0, 0))
    return pl.pallas_call(
        _ln_kernel, grid=(n // TILE_ROWS,), in_specs=[row, vec, vec], out_specs=row,
        out_shape=jax.ShapeDtypeStruct((n, d), F32), compiler_params=_params("parallel"),
        name="ln_in")(x, g.reshape(1, d), b.reshape(1, d))


def _proj_kernel(h_ref, w_ref, c64_ref, s64_ref, c128_ref, s128_ref,
                 qa_ref, kaf_ref, kab_ref, vaf_ref, vab_ref, qb_ref, kbf_ref, kbb_ref,
                 vbf_ref, vbb_ref, qi_ref, kiwi_ref, ki4_ref):
    hb = h_ref[...].astype(BF16)
    c64, s64, c128, s128 = c64_ref[...], s64_ref[...], c128_ref[...], s128_ref[...]
    lane = lax.broadcasted_iota(I32, c64.shape, 1)
    low_half = (lane & (A_HEAD_DIM // 2)) == 0

    def rope64(y):
        swapped = jnp.where(low_half, pltpu.roll(y, LANES - 32, 1), pltpu.roll(y, 32, 1))
        return y * c64 + swapped * s64

    def rope128(y):
        return y * c128 + pltpu.roll(y, 64, 1) * s128

    def proj(col, width):
        return _dot(hb, w_ref[:, col:col + width])

    def tiles(y):
        return [y[:, t * LANES:(t + 1) * LANES] for t in range(y.shape[1] // LANES)]

    for t, y in enumerate(tiles(proj(COL_QA, 1024))):
        qa_ref[:, t * LANES:(t + 1) * LANES] = rope64(y).astype(BF16)
    for t, y in enumerate(tiles(proj(COL_KA, 1024))):
        r = rope64(y)
        kaf_ref[:, t * LANES:(t + 1) * LANES] = r
        kab_ref[:, t * LANES:(t + 1) * LANES] = r.astype(BF16)
    y = proj(COL_VA, 1024)
    vaf_ref[...] = y
    vab_ref[...] = y.astype(BF16)
    for t, y in enumerate(tiles(proj(COL_QB, 1024))):
        qb_ref[:, t * LANES:(t + 1) * LANES] = rope128(y).astype(BF16)
    for t, y in enumerate(tiles(proj(COL_KB, 256))):
        r = rope128(y)
        kbf_ref[:, t * LANES:(t + 1) * LANES] = r
        kbb_ref[:, t * LANES:(t + 1) * LANES] = r.astype(BF16)
    y = proj(COL_VB, 256)
    vbf_ref[...] = y
    vbb_ref[...] = y.astype(BF16)
    for t, y in enumerate(tiles(proj(COL_QI, 256))):
        qi_ref[:, t * LANES:(t + 1) * LANES] = rope64(y).astype(BF16)
    y = proj(COL_KIWI, LANES)
    is_key = lane < IDX_DIM
    kiwi = jnp.where(is_key, rope64(y), y * (IDX_HEADS ** -0.5))
    kiwi_ref[...] = kiwi
    ki2 = jnp.where(is_key, kiwi, pltpu.roll(kiwi, IDX_DIM, 1)).astype(BF16)
    ki4_ref[:, 0:LANES] = ki2
    ki4_ref[:, LANES:2 * LANES] = ki2


def _project(h, w1, tabs):
    n = h.shape[0]

    def row(width):
        return pl.BlockSpec((TILE_ROWS, width), lambda i: (i, 0))

    outs = [(1024, BF16), (1024, F32), (1024, BF16), (1024, F32), (1024, BF16), (1024, BF16),
            (256, F32), (256, BF16), (256, F32), (256, BF16), (256, BF16), (LANES, F32), (256, BF16)]
    return pl.pallas_call(
        _proj_kernel, grid=(n // TILE_ROWS,),
        in_specs=[row(D_MODEL), pl.BlockSpec((D_MODEL, N_PROJ), lambda i: (0, 0))] + [row(LANES)] * 4,
        out_specs=[row(w) for w, _ in outs],
        out_shape=[jax.ShapeDtypeStruct((n, w), dt) for w, dt in outs],
        compiler_params=_params("parallel"), name="in_proj")(h, w1, *tabs)


def _subln(o1, o2, lam, g, lam_init):
    o = o1 - lam * o2
    y = o * lax.rsqrt(jnp.mean(o * o, -1, keepdims=True) + LN_EPS) * g
    return y * (1.0 - lam_init)


def _diff_p_kernel(lam_ref, q_ref, k_ref, v_ref, g_ref, o_ref, m_sc, l_sc, acc_sc, *, lam_init):
    tq = q_ref.shape[0]
    i = pl.program_id(1)
    q = q_ref[...]
    lane = lax.broadcasted_iota(I32, (1, LANES), 1)
    zero = jnp.zeros_like(q)
    qz = (jnp.where(lane < A_HEAD_DIM, q, zero), jnp.where(lane >= A_HEAD_DIM, q, zero))
    q_chunk = (i * tq + lax.broadcasted_iota(I32, (tq, 1), 0)) >> 6
    m_sc[...] = jnp.full(m_sc.shape, NEG_INF, F32)
    l_sc[...] = jnp.zeros(l_sc.shape, F32)
    acc_sc[...] = jnp.zeros(acc_sc.shape, F32)
    n_kv = ((i + 1) * tq + TILE_K - 1) // TILE_K

    def body(j, carry):
        rows = pl.ds(pl.multiple_of(j * TILE_K, TILE_K), TILE_K)
        k = k_ref[rows, :]
        v = v_ref[rows, :]
        k_idx = j * TILE_K + lax.broadcasted_iota(I32, (1, TILE_K), 1)
        k_chunk = jnp.where(k_idx >= FRONT_PAD, k_idx >> 6, BIG_CHUNK)
        mask = k_chunk <= q_chunk
        for mp in range(2):
            s = jnp.where(mask, _dot_nt(qz[mp], k) * (A_HEAD_DIM ** -0.5), NEG_INF)
            m_prev = m_sc[mp]
            m_new = jnp.maximum(m_prev, jnp.max(s, axis=1, keepdims=True))
            alpha = jnp.exp(m_prev - m_new)
            p = jnp.exp(s - m_new)
            l_sc[mp] = alpha * l_sc[mp] + jnp.sum(p, axis=1, keepdims=True)
            acc_sc[mp] = alpha * acc_sc[mp] + _dot(p.astype(BF16), v)
            m_sc[mp] = m_new
        return carry

    lax.fori_loop(0, n_kv, body, 0)
    y = _subln(acc_sc[0] / l_sc[0], acc_sc[1] / l_sc[1], lam_ref[0], g_ref[...], lam_init)
    o_ref[...] = y.astype(BF16)


def _diff_attn_prompt(lam, qa, ka, va, g, lp, lam_init):
    n = qa.shape[0]
    kv = pl.BlockSpec((lp, LANES), lambda h, i: (0, h))
    qo = pl.BlockSpec((TILE_Q_A, LANES), lambda h, i: (i, h))
    return pl.pallas_call(
        functools.partial(_diff_p_kernel, lam_init=lam_init),
        grid=(A_HEADS, lp // TILE_Q_A),
        in_specs=[pl.BlockSpec(memory_space=pltpu.SMEM), qo, kv, kv,
                  pl.BlockSpec((1, LANES), lambda h, i: (0, 0))],
        out_specs=qo,
        out_shape=jax.ShapeDtypeStruct((n, A_HEADS * LANES), BF16),
        scratch_shapes=[pltpu.VMEM((2, TILE_Q_A, 1), F32), pltpu.VMEM((2, TILE_Q_A, 1), F32),
                        pltpu.VMEM((2, TILE_Q_A, LANES), F32)],
        compiler_params=_params("parallel", "parallel"), name="diff_attn_prompt")(lam, qa, ka, va, g)


def _diff_s_kernel(lam_ref, q_ref, ck_ref, cv_ref, kn_ref, vn_ref, g_ref, prev_ref, o_ref, *, lam_init, n_keys):
    del prev_ref
    ds_ = q_ref.shape[0]
    past = ck_ref.shape[0]
    pad = n_keys - past - ds_
    lane = lax.broadcasted_iota(I32, (1, LANES), 1)
    valid = lax.broadcasted_iota(I32, (1, n_keys), 1) < past + ds_
    for h in range(A_HEADS):
        cols = slice(h * LANES, (h + 1) * LANES)
        q = q_ref[:, cols]
        zero = jnp.zeros_like(q)
        k = jnp.concatenate([ck_ref[:, cols].astype(BF16), kn_ref[:, cols], jnp.zeros((pad, LANES), BF16)], 0)
        v = jnp.concatenate([cv_ref[:, cols].astype(BF16), vn_ref[:, cols], jnp.zeros((pad, LANES), BF16)], 0)
        outs = []
        for qz in (jnp.where(lane < A_HEAD_DIM, q, zero), jnp.where(lane >= A_HEAD_DIM, q, zero)):
            s = jnp.where(valid, _dot_nt(qz, k) * (A_HEAD_DIM ** -0.5), NEG_INF)
            p = jnp.exp(s - jnp.max(s, axis=1, keepdims=True))
            outs.append(_dot(p.astype(BF16), v) / jnp.sum(p, axis=1, keepdims=True))
        o_ref[:, cols] = _subln(outs[0], outs[1], lam_ref[0], g_ref[...], lam_init).astype(BF16)


def _diff_attn_sample(lam, qa, cache_k, cache_v, ka, va, g, ya, layer, lp, lam_init):
    _, nb, past, width = cache_k.shape
    n = qa.shape[0]
    ds_ = (n - lp) // nb
    n_keys = -(-(past + ds_) // LANES) * LANES
    base = lp // ds_
    new = pl.BlockSpec((ds_, width), lambda b: (base + b, 0))
    cache = pl.BlockSpec((None, None, past, width), lambda b: (layer, b, 0, 0))
    return pl.pallas_call(
        functools.partial(_diff_s_kernel, lam_init=lam_init, n_keys=n_keys), grid=(nb,),
        in_specs=[pl.BlockSpec(memory_space=pltpu.SMEM), new, cache, cache, new, new,
                  pl.BlockSpec((1, LANES), lambda b: (0, 0)), pl.BlockSpec(memory_space=pl.ANY)],
        out_specs=new, out_shape=jax.ShapeDtypeStruct(ya.shape, ya.dtype),
        input_output_aliases={7: 0},
        compiler_params=_params("parallel"), name="diff_attn_sample")(lam, qa, cache_k, cache_v, ka, va, g, ya)


def _index_scores(qiz, w_cols, ki4, rows):
    r = _dot_nt(qiz, ki4)
    sc = w_cols[0] * jnp.maximum(r[0:rows], 0.0)
    for h in range(1, IDX_HEADS):
        sc = sc + w_cols[h] * jnp.maximum(r[h * rows:(h + 1) * rows], 0.0)
    return sc


def _stack_idx_queries(qi, rows):
    lane = lax.broadcasted_iota(I32, (1, IDX_HEADS * IDX_DIM), 1)
    zero = jnp.zeros_like(qi)
    return [jnp.where((lane >= IDX_DIM * h) & (lane < IDX_DIM * (h + 1)), qi, zero) for h in range(IDX_HEADS)]


def _selected(key, t, need, seen, u):
    eq = key == t
    prefix = _dot(jnp.where(eq, 1.0, 0.0).astype(BF16), u)
    sel = ((key > t) | (eq & (seen + prefix <= need))) & (key > KEY_HALF_NEG)
    return sel, seen + prefix[:, -1:]


def _dsa_p_kernel(qb_ref, qi_ref, kiwi_ref, kb_ref, vb_ref, ki4_ref, u_ref, o_ref,
                  key_sc, qbs_sc, qiz_sc, m_sc, l_sc, acc_sc, *, n_sel):
    tq = qb_ref.shape[0]
    i = pl.program_id(0)
    n_kv = ((i + 1) * tq + TILE_K - 1) // TILE_K
    for c in range(B_KV_HEADS):
        for g in range(B_GROUP):
            hh = c * B_GROUP + g
            qbs_sc[c, g * tq:(g + 1) * tq, :] = qb_ref[:, hh * LANES:(hh + 1) * LANES]
    for h, qh in enumerate(_stack_idx_queries(qi_ref[...], tq)):
        qiz_sc[h * tq:(h + 1) * tq, :] = qh
    wq = kiwi_ref[...] * (IDX_DIM ** -0.5)
    w_cols = [wq[:, IDX_DIM + h:IDX_DIM + h + 1] for h in range(IDX_HEADS)]
    q_chunk = (i * tq + lax.broadcasted_iota(I32, (tq, 1), 0)) >> 6

    def tile_rows(j):
        return pl.ds(pl.multiple_of(j * TILE_K, TILE_K), TILE_K)

    def score_tile(j, carry):
        sc = _index_scores(qiz_sc[...], w_cols, ki4_ref[tile_rows(j), :], tq)
        k_idx = j * TILE_K + lax.broadcasted_iota(I32, (1, TILE_K), 1)
        k_chunk = jnp.where(k_idx >= FRONT_PAD, k_idx >> 6, BIG_CHUNK)
        key_sc[j] = _sort_key(jnp.where(k_chunk <= q_chunk, sc, NEG_INF))
        return carry

    lax.fori_loop(0, n_kv, score_tile, 0)

    def count_ge(c):
        def body(j, acc):
            return acc + jnp.sum(jnp.where(key_sc[j] >= c, 1.0, 0.0), axis=1, keepdims=True)
        return lax.fori_loop(0, n_kv, body, jnp.zeros((tq, 1), F32))

    t = _kth_largest_key(count_ge, tq, float(n_sel))
    need = float(n_sel) - count_ge(t + 1)

    m_sc[...] = jnp.full(m_sc.shape, NEG_INF, F32)
    l_sc[...] = jnp.zeros(l_sc.shape, F32)
    acc_sc[...] = jnp.zeros(acc_sc.shape, F32)

    def attend(j, seen):
        sel, seen = _selected(key_sc[j], t, need, seen, u_ref[...])
        kt = kb_ref[tile_rows(j), :]
        vt = vb_ref[tile_rows(j), :]
        for c in range(B_KV_HEADS):
            cols = slice(c * LANES, (c + 1) * LANES)
            s_all = _dot_nt(qbs_sc[c], kt[:, cols]) * (B_HEAD_DIM ** -0.5)
            for g in range(B_GROUP):
                rows = slice(g * tq, (g + 1) * tq)
                s = jnp.where(sel, s_all[rows], NEG_INF)
                m_prev = m_sc[c, rows]
                m_new = jnp.maximum(m_prev, jnp.max(s, axis=1, keepdims=True))
                alpha = jnp.exp(m_prev - m_new)
                p = jnp.exp(s - m_new)
                l_sc[c, rows] = alpha * l_sc[c, rows] + jnp.sum(p, axis=1, keepdims=True)
                acc_sc[c, rows] = alpha * acc_sc[c, rows] + _dot(p.astype(BF16), vt[:, cols])
                m_sc[c, rows] = m_new
        return seen

    lax.fori_loop(0, n_kv, attend, jnp.zeros((tq, 1), F32))
    for c in range(B_KV_HEADS):
        for g in range(B_GROUP):
            hh = c * B_GROUP + g
            rows = slice(g * tq, (g + 1) * tq)
            o_ref[:, hh * LANES:(hh + 1) * LANES] = (acc_sc[c, rows] / l_sc[c, rows]).astype(BF16)


def _dsa_prompt(qb, qi, kiwi, kb, vb, ki4, u, lp, n_sel):
    n = qb.shape[0]
    tq = TILE_Q_B

    def row(width):
        return pl.BlockSpec((tq, width), lambda i: (i, 0))

    def resident(width):
        return pl.BlockSpec((lp, width), lambda i: (0, 0), pipeline_mode=pl.Buffered(1))

    return pl.pallas_call(
        functools.partial(_dsa_p_kernel, n_sel=n_sel), grid=(lp // tq,),
        in_specs=[row(1024), row(256), row(LANES), resident(256), resident(256), resident(256),
                  pl.BlockSpec((TILE_K, TILE_K), lambda i: (0, 0), pipeline_mode=pl.Buffered(1))],
        out_specs=row(1024), out_shape=jax.ShapeDtypeStruct((n, 1024), BF16),
        scratch_shapes=[pltpu.VMEM((lp // TILE_K, tq, TILE_K), I32),
                        pltpu.VMEM((B_KV_HEADS, B_GROUP * tq, LANES), BF16),
                        pltpu.VMEM((IDX_HEADS * tq, IDX_HEADS * IDX_DIM), BF16),
                        pltpu.VMEM((B_KV_HEADS, B_GROUP * tq, 1), F32),
                        pltpu.VMEM((B_KV_HEADS, B_GROUP * tq, 1), F32),
                        pltpu.VMEM((B_KV_HEADS, B_GROUP * tq, LANES), F32)],
        compiler_params=_params("parallel"), name="dsa_prompt")(qb, qi, kiwi, kb, vb, ki4, u)


def _dsa_s_kernel(qb_ref, qi_ref, kiwi_ref, ckb_ref, cvb_ref, cki_ref, kbn_ref, vbn_ref, ki4n_ref,
                  rep_ref, u_ref, prev_ref, o_ref, *, n_sel, n_keys):
    del prev_ref
    tq = qb_ref.shape[0]
    past = ckb_ref.shape[0]
    pad = n_keys - past - tq
    width = B_KV_HEADS * B_HEAD_DIM
    kb = jnp.concatenate([ckb_ref[...].astype(BF16), kbn_ref[...], jnp.zeros((pad, width), BF16)], 0)
    vb = jnp.concatenate([cvb_ref[...].astype(BF16), vbn_ref[...], jnp.zeros((pad, width), BF16)], 0)
    ki4_past = _dot(cki_ref[...].astype(BF16), rep_ref[...]).astype(BF16)
    ki4 = jnp.concatenate([ki4_past, ki4n_ref[...], jnp.zeros((pad, width), BF16)], 0)
    qiz = jnp.concatenate(_stack_idx_queries(qi_ref[...], tq), 0)
    wq = kiwi_ref[...] * (IDX_DIM ** -0.5)
    w_cols = [wq[:, IDX_DIM + h:IDX_DIM + h + 1] for h in range(IDX_HEADS)]
    sc = _index_scores(qiz, w_cols, ki4, tq)
    valid = lax.broadcasted_iota(I32, (1, n_keys), 1) < past + tq
    key = _sort_key(jnp.where(valid, sc, NEG_INF))

    def count_ge(c):
        return jnp.sum(jnp.where(key >= c, 1.0, 0.0), axis=1, keepdims=True)

    t = _kth_largest_key(count_ge, tq, float(n_sel))
    need = float(n_sel) - count_ge(t + 1)
    sel, _ = _selected(key, t, need, jnp.zeros((tq, 1), F32), u_ref[...])
    for c in range(B_KV_HEADS):
        cols = slice(c * LANES, (c + 1) * LANES)
        q = jnp.concatenate([qb_ref[:, (c * B_GROUP + g) * LANES:(c * B_GROUP + g + 1) * LANES]
                             for g in range(B_GROUP)], 0)
        s_all = _dot_nt(q, kb[:, cols]) * (B_HEAD_DIM ** -0.5)
        for g in range(B_GROUP):
            hh = c * B_GROUP + g
            s = jnp.where(sel, s_all[g * tq:(g + 1) * tq], NEG_INF)
            p = jnp.exp(s - jnp.max(s, axis=1, keepdims=True))
            o = _dot(p.astype(BF16), vb[:, cols]) / jnp.sum(p, axis=1, keepdims=True)
            o_ref[:, hh * LANES:(hh + 1) * LANES] = o.astype(BF16)


def _dsa_sample(qb, qi, kiwi, cache_k, cache_v, cache_idx, kb, vb, ki4, rep, u, yb, layer, lp, n_sel):
    _, nb, past, _ = cache_k.shape
    n = qb.shape[0]
    ds_ = (n - lp) // nb
    n_keys = u.shape[0]
    base = lp // ds_

    def new(width):
        return pl.BlockSpec((ds_, width), lambda b: (base + b, 0))

    def cache(width):
        return pl.BlockSpec((None, None, past, width), lambda b: (layer, b, 0, 0))

    def const(shape):
        return pl.BlockSpec(shape, lambda b: (0, 0))

    return pl.pallas_call(
        functools.partial(_dsa_s_kernel, n_sel=n_sel, n_keys=n_keys), grid=(nb,),
        in_specs=[new(1024), new(256), new(LANES), cache(256), cache(256), cache(IDX_DIM),
                  new(256), new(256), new(256), const(rep.shape), const(u.shape),
                  pl.BlockSpec(memory_space=pl.ANY)],
        out_specs=new(1024), out_shape=jax.ShapeDtypeStruct(yb.shape, yb.dtype),
        input_output_aliases={11: 0},
        compiler_params=_params("parallel"), name="dsa_sample")(
            qb, qi, kiwi, cache_k, cache_v, cache_idx, kb, vb, ki4, rep, u, yb)


def _merge_kernel(h_ref, ya_ref, yb_ref, wga_ref, wgb_ref, wpa_ref, wpb_ref, wo_ref, g_ref, b_ref, o_ref):
    h = h_ref[...]
    hb = h.astype(BF16)
    mixed = (jax.nn.sigmoid(_dot(hb, wga_ref[...])) * _dot(ya_ref[...], wpa_ref[...])
             + jax.nn.sigmoid(_dot(hb, wgb_ref[...])) * _dot(yb_ref[...], wpb_ref[...]))
    y = _dot(mixed.astype(BF16), wo_ref[...])
    o_ref[...] = _layernorm(DEEPNORM_ALPHA * h + y, g_ref[...], b_ref[...])


def _merge(h, ya, yb, wga, wgb, wpa, wpb, wo, g, b):
    n, d = h.shape
    row = pl.BlockSpec((TILE_ROWS, d), lambda i: (i, 0))
    mat = pl.BlockSpec((d, d), lambda i: (0, 0))
    vec = pl.BlockSpec((1, d), lambda i: (0, 0))
    return pl.pallas_call(
        _merge_kernel, grid=(n // TILE_ROWS,), in_specs=[row, row, row] + [mat] * 5 + [vec, vec],
        out_specs=row, out_shape=jax.ShapeDtypeStruct((n, d), F32),
        compiler_params=_params("parallel"), name="merge_ln1")(
            h, ya, yb, wga, wgb, wpa, wpb, wo, g.reshape(1, d), b.reshape(1, d))


def _router_kernel(h_ref, wr_ref, bias_ref, gates_ref):
    s = jax.nn.sigmoid(_dot(h_ref[...].astype(BF16), wr_ref[...]))
    sel = s + bias_ref[...]
    lane = lax.broadcasted_iota(I32, (1, N_EXPERTS), 1)
    lane_f = lane.astype(F32)
    group = lane >> 3
    neg = -jnp.inf

    def first_argmax(x, m):
        return jnp.min(jnp.where(x == m, lane_f, float(N_EXPERTS)), axis=1, keepdims=True)

    g_score = []
    for g in range(N_GROUPS):
        x = jnp.where(group == g, sel, neg)
        m1 = jnp.max(x, axis=1, keepdims=True)
        x = jnp.where(lane_f == first_argmax(x, m1), neg, x)
        g_score.append(m1 + jnp.max(x, axis=1, keepdims=True))
    e_keep = None
    for g in range(N_GROUPS):
        ahead = jnp.zeros_like(g_score[g])
        for o in range(N_GROUPS):
            if o != g:
                beats = (g_score[o] > g_score[g]) | ((g_score[o] == g_score[g]) & (o < g))
                ahead = ahead + jnp.where(beats, 1.0, 0.0)
        keep = (group == g) & (ahead < float(TOPK_GROUPS))
        e_keep = keep if e_keep is None else (e_keep | keep)
    x = jnp.where(e_keep, sel, NEG_INF)
    chosen = None
    for _ in range(TOP_K):
        hit = lane_f == first_argmax(x, jnp.max(x, axis=1, keepdims=True))
        chosen = hit if chosen is None else (chosen | hit)
        x = jnp.where(hit, neg, x)
    w = jnp.where(chosen, s, 0.0)
    gates_ref[...] = w / jnp.sum(w, axis=1, keepdims=True) * ROUTE_SCALE


def _router(h, wr, bias):
    n, d = h.shape
    return pl.pallas_call(
        _router_kernel, grid=(n // TILE_ROWS,),
        in_specs=[pl.BlockSpec((TILE_ROWS, d), lambda i: (i, 0)),
                  pl.BlockSpec((d, N_EXPERTS), lambda i: (0, 0)),
                  pl.BlockSpec((1, N_EXPERTS), lambda i: (0, 0))],
        out_specs=pl.BlockSpec((TILE_ROWS, N_EXPERTS), lambda i: (i, 0)),
        out_shape=jax.ShapeDtypeStruct((n, N_EXPERTS), F32),
        compiler_params=_params("parallel"), name="router")(h, wr, bias.reshape(1, N_EXPERTS))


def _silu(x):
    return x * jax.nn.sigmoid(x)


def _moe_kernel(h_ref, gates_ref, wg_ref, wu_ref, wd_ref, sg_ref, su_ref, sd_ref, g_ref, b_ref, o_ref,
                xb_sc, acc_sc):
    step = pl.program_id(1)

    @pl.when(step == 0)
    def _():
        xb = h_ref[...].astype(BF16)
        xb_sc[...] = xb
        a = _silu(_dot(xb, sg_ref[...])) * _dot(xb, su_ref[...])
        acc_sc[...] = _dot(a.astype(BF16), sd_ref[...])

    xb = xb_sc[...]
    gates = gates_ref[...]
    lane = lax.broadcasted_iota(I32, (1, N_EXPERTS), 1)
    acts = []
    for e in range(EXPERT_BLOCK):
        gate = jnp.sum(jnp.where(lane == step * EXPERT_BLOCK + e, gates, 0.0), axis=1, keepdims=True)
        a = _silu(_dot(xb, wg_ref[e])) * _dot(xb, wu_ref[e]) * gate
        acts.append(a.astype(BF16))
    wd = wd_ref[...]
    acc_sc[...] += _dot(jnp.concatenate(acts, axis=1), wd.reshape(wd.shape[0] * wd.shape[1], wd.shape[2]))

    @pl.when(step == pl.num_programs(1) - 1)
    def _():
        o_ref[...] = _layernorm(DEEPNORM_ALPHA * h_ref[...] + acc_sc[...], g_ref[...], b_ref[...])


def _moe(h, gates, wg, wu, wd, sg, su, sd, g, b):
    n, d = h.shape
    f = wg.shape[-1]
    row = pl.BlockSpec((TILE_MOE, d), lambda r, e: (r, 0))
    vec = pl.BlockSpec((1, d), lambda r, e: (0, 0))
    return pl.pallas_call(
        _moe_kernel, grid=(n // TILE_MOE, N_EXPERTS // EXPERT_BLOCK),
        in_specs=[row, pl.BlockSpec((TILE_MOE, N_EXPERTS), lambda r, e: (r, 0)),
                  pl.BlockSpec((EXPERT_BLOCK, d, f), lambda r, e: (e, 0, 0)),
                  pl.BlockSpec((EXPERT_BLOCK, d, f), lambda r, e: (e, 0, 0)),
                  pl.BlockSpec((EXPERT_BLOCK, f, d), lambda r, e: (e, 0, 0)),
                  pl.BlockSpec((d, sg.shape[1]), lambda r, e: (0, 0)),
                  pl.BlockSpec((d, sg.shape[1]), lambda r, e: (0, 0)),
                  pl.BlockSpec((sg.shape[1], d), lambda r, e: (0, 0)), vec, vec],
        out_specs=row, out_shape=jax.ShapeDtypeStruct((n, d), F32),
        scratch_shapes=[pltpu.VMEM((TILE_MOE, d), BF16), pltpu.VMEM((TILE_MOE, d), F32)],
        compiler_params=_params("parallel", "arbitrary"), name="moe_ln2")(
            h, gates, wg, wu, wd, sg, su, sd, g.reshape(1, d), b.reshape(1, d))


def _rope_tables(pos):
    def table(half, reps):
        inv = ROPE_THETA ** (-jnp.arange(half, dtype=F32) / half)
        ang = pos.astype(F32)[:, None] * inv[None, :]
        cos, sin = jnp.cos(ang), jnp.sin(ang)
        return (jnp.tile(jnp.concatenate([cos, cos], -1), (1, reps)),
                jnp.tile(jnp.concatenate([-sin, sin], -1), (1, reps)))
    c64, s64 = table(A_HEAD_DIM // 2, 2)
    c128, s128 = table(B_HEAD_DIM // 2, 1)
    return c64, s64, c128, s128


def kernel(x_prompt, x_sample, cache_a_k, cache_a_v, cache_b_k, cache_b_v, cache_b_idx, meta_tokens, ln_in_g, ln_in_b, w_in, lam_q1, lam_k1, lam_q2, lam_k2, subln_g, w_proj_a, w_proj_b, w_out, ln1_g, ln1_b, w_router, router_bias, w_expert_gate, w_expert_up, w_expert_down, w_shared_gate, w_shared_up, w_shared_down, ln2_g, ln2_b):
    batch, seq, d = x_prompt.shape
    nb, ds_, _ = x_sample.shape
    depth, _, past = cache_a_k.shape[:3]
    assert batch == 1 and d == D_MODEL and seq % TILE_K == 0 and ds_ == CHUNK
    lp = FRONT_PAD + N_META + seq
    ns = nb * ds_
    n = lp + ns
    assert n % TILE_MOE == 0 and past % 16 == 0
    n_sel_p = min(TOPK_MAX, seq // 4)
    n_sel_s = min(TOPK_MAX, (past + ds_) // 4)
    n_keys_s = -(-(past + ds_) // LANES) * LANES

    x_all = jnp.concatenate([jnp.zeros((FRONT_PAD, d), F32), meta_tokens.astype(F32), x_prompt[0],
                             x_sample.reshape(ns, d)], 0)
    pos = jnp.concatenate([jnp.maximum(jnp.arange(lp, dtype=jnp.int32) - FRONT_PAD, 0),
                           jnp.tile(past + jnp.arange(ds_, dtype=jnp.int32), nb)])
    tabs = _rope_tables(pos)
    u_p = jnp.triu(jnp.ones((TILE_K, TILE_K), BF16))
    u_s = jnp.triu(jnp.ones((n_keys_s, n_keys_s), BF16))
    rep = jnp.tile(jnp.eye(IDX_DIM, dtype=BF16), (1, IDX_HEADS))
    ck_a = cache_a_k.reshape(depth, nb, past, -1)
    cv_a = cache_a_v.reshape(depth, nb, past, -1)
    ck_b = cache_b_k.reshape(depth, nb, past, -1)
    cv_b = cache_b_v.reshape(depth, nb, past, -1)

    h = _ln_rows(x_all, ln_in_g, ln_in_b)
    rows = []
    for l in range(depth):
        lam_init = 0.8 - 0.6 * math.exp(-0.3 * l)
        lam = (jnp.exp(jnp.sum(lam_q1[l].astype(F32) * lam_k1[l].astype(F32)))
               - jnp.exp(jnp.sum(lam_q2[l].astype(F32) * lam_k2[l].astype(F32))) + lam_init).reshape(1)
        w1 = jnp.concatenate([w_in[l][:, :N_RAW_PROJ], jnp.zeros((d, N_PROJ - N_RAW_PROJ), F32)], 1).astype(BF16)
        wga = w_in[l][:, N_RAW_PROJ:N_RAW_PROJ + d].astype(BF16)
        wgb = w_in[l][:, N_RAW_PROJ + d:].astype(BF16)
        (qa, ka_f, ka, va_f, va, qb, kb_f, kb, vb_f, vb, qi, kiwi, ki4) = _project(h, w1, tabs)
        rows.append((ka_f, va_f, kb_f, vb_f, kiwi))
        g = subln_g[l].reshape(1, LANES).astype(F32)
        ya = _diff_attn_prompt(lam, qa, ka, va, g, lp, lam_init)
        ya = _diff_attn_sample(lam, qa, ck_a, cv_a, ka, va, g, ya, l, lp, lam_init)
        yb = _dsa_prompt(qb, qi, kiwi, kb, vb, ki4, u_p, lp, n_sel_p)
        yb = _dsa_sample(qb, qi, kiwi, ck_b, cv_b, cache_b_idx, kb, vb, ki4, rep, u_s, yb, l, lp, n_sel_s)
        h = _merge(h, ya, yb, wga, wgb, w_proj_a[l].astype(BF16), w_proj_b[l].astype(BF16),
                   w_out[l].astype(BF16), ln1_g[l], ln1_b[l])
        gates = _router(h, w_router[l].astype(BF16), router_bias[l].astype(F32))
        h = _moe(h, gates, w_expert_gate[l].astype(BF16), w_expert_up[l].astype(BF16),
                 w_expert_down[l].astype(BF16), w_shared_gate[l].astype(BF16), w_shared_up[l].astype(BF16),
                 w_shared_down[l].astype(BF16), ln2_g[l], ln2_b[l])

    p0 = FRONT_PAD
    y_prompt = h[p0 + N_META:lp].reshape(1, seq, d)
    y_sample = h[lp:].reshape(nb, ds_, d)

    def stack(idx, width, tail):
        full = jnp.stack([r[idx][:, :width] for r in rows])
        return (full[:, p0:lp].reshape((depth, 1, lp - p0) + tail),
                full[:, lp:].reshape((depth, nb, ds_) + tail))

    ak_p, ak_s = stack(0, 1024, (2 * A_HEADS, A_HEAD_DIM))
    av_p, av_s = stack(1, 1024, (A_HEADS, 2 * A_HEAD_DIM))
    bk_p, bk_s = stack(2, 256, (B_KV_HEADS, B_HEAD_DIM))
    bv_p, bv_s = stack(3, 256, (B_KV_HEADS, B_HEAD_DIM))
    bi_p, bi_s = stack(4, IDX_DIM, (IDX_DIM,))
    return (y_prompt, y_sample, ak_p, av_p, bk_p, bv_p, bi_p, ak_s, av_s, bk_s, bv_s, bi_s)
```

```python
import functools
import math

import numpy as np
import jax
import jax.numpy as jnp
from jax import lax
from jax.experimental import pallas as pl
from jax.experimental.pallas import tpu as pltpu

F32 = jnp.float32
BF16 = jnp.bfloat16
I32 = jnp.int32

MODEL_DEPTH = 4
CHUNK = 64
N_META = 16
ROPE_THETA = 10000.0
LN_EPS = 1e-5
NEG_INF = -1e30
DEEPNORM_ALPHA = (2 * MODEL_DEPTH) ** 0.25
A_HEADS = 8
A_HEAD_DIM = 64
B_HEADS = 8
B_KV_HEADS = 2
B_GROUP = B_HEADS // B_KV_HEADS
B_HEAD_DIM = 128
IDX_HEADS = 4
IDX_DIM = 64
TOPK_MAX = 256
N_EXPERTS = 64
TOP_K = 8
N_GROUPS = 8
GROUP_SIZE = N_EXPERTS // N_GROUPS
TOPK_GROUPS = 4
ROUTE_SCALE = 2.5

LANES = 128
D_MODEL = 1024
N_PROJ = 4992
COL_QA, COL_KA, COL_VA, COL_QB, COL_KB, COL_VB, COL_QI, COL_KIWI = (
    0, 1024, 2048, 3072, 4096, 4352, 4608, 4864)
N_RAW_PROJ = 4932

TILE_K = 512
FRONT_PAD = TILE_K - N_META
TILE_Q_A = 256
TILE_Q_B = 256
TILE_ROWS = 256
TILE_MOE = 512
EXPERT_BLOCK = 4
VMEM_LIMIT = 56 * 1024 * 1024
BIG_CHUNK = 1 << 30
INT_MIN = -(2 ** 31)
LOG2E = 1.4426950408889634
V_ROWS = 144
VT_PER_K = TILE_K // TILE_ROWS
COUNT_ROWS = 32


def _f32_key(v):
    b = int(np.float32(v).view(np.int32))
    return b ^ ((b >> 31) & 0x7FFFFFFF)


KEY_HALF_NEG = _f32_key(0.5 * NEG_INF)


def _dot(a, b):
    return jnp.dot(a, b, preferred_element_type=F32)


def _dot_nt(a, b):
    return lax.dot_general(a, b, (((1,), (1,)), ((), ())), preferred_element_type=F32)


def _layernorm(x, g, b):
    mu = jnp.mean(x, -1, keepdims=True)
    xc = x - mu
    var = jnp.mean(xc * xc, -1, keepdims=True)
    return xc * lax.rsqrt(var + LN_EPS) * g + b


def _sort_key(x):
    bits = lax.bitcast_convert_type(x, I32)
    return bits ^ ((bits >> 31) & 0x7FFFFFFF)


def _kth_largest_key(count_ge, shape, k):
    def body(b, tu):
        cand = tu | jnp.left_shift(jnp.int32(1), 31 - b)
        c = count_ge(cand ^ INT_MIN)
        return jnp.where(c >= k, cand, tu)
    tu = lax.fori_loop(0, 32, body, jnp.zeros(shape, I32))
    return tu ^ INT_MIN


def _tile_rows(j):
    return pl.ds(pl.multiple_of(j * TILE_K, TILE_K), TILE_K)


def _chunk_mask_t(j, q_chunk):
    k_idx = j * TILE_K + lax.broadcasted_iota(I32, (TILE_K, 1), 0)
    return jnp.where(k_idx >= FRONT_PAD, k_idx >> 6, BIG_CHUNK) <= q_chunk


def _flash_update_t(t_ref, m_ref, acc_ref, vt_tiles, width, select=None):
    for b in range(t_ref.shape[1] // width):
        cols = slice(b * width, (b + 1) * width)
        t = t_ref[:, cols]
        if select is not None:
            t = jnp.where(select, t, NEG_INF)
        m_prev = m_ref[:, cols]
        m_new = jnp.maximum(m_prev, jnp.max(t, axis=0, keepdims=True))
        p = jnp.exp2(t - m_new).astype(BF16)
        pv = _dot(vt_tiles[0], p[0:TILE_ROWS])
        for c in range(1, VT_PER_K):
            pv = pv + _dot(vt_tiles[c], p[c * TILE_ROWS:(c + 1) * TILE_ROWS])
        acc_ref[:, cols] = jnp.exp2(m_prev - m_new) * acc_ref[:, cols] + pv
        m_ref[:, cols] = m_new


def _params(*sem):
    return pltpu.CompilerParams(dimension_semantics=sem, vmem_limit_bytes=VMEM_LIMIT)


def _ln_kernel(x_ref, g_ref, b_ref, o_ref):
    o_ref[...] = _layernorm(x_ref[...], g_ref[...], b_ref[...])


def _ln_rows(x, g, b):
    n, d = x.shape
    row = pl.BlockSpec((TILE_ROWS, d), lambda i: (i, 0))
    vec = pl.BlockSpec((1, d), lambda i: (0, 0))
    return pl.pallas_call(
        _ln_kernel, grid=(n // TILE_ROWS,), in_specs=[row, vec, vec], out_specs=row,
        out_shape=jax.ShapeDtypeStruct((n, d), F32), compiler_params=_params("parallel"),
        name="ln_in")(x, g.reshape(1, d), b.reshape(1, d))


def _proj_kernel(h_ref, w_ref, c64_ref, s64_ref, c128_ref, s128_ref,
                 qa_ref, qat_ref, kaf_ref, kab_ref, vaf_ref, vab_ref, vat_ref, qb_ref, qbt_ref,
                 kbf_ref, kbb_ref, vbf_ref, vbb_ref, vbt_ref, qi_ref, qit_ref, kiwi_ref, ki_ref,
                 ki4_ref, wt_ref):
    hb = h_ref[...].astype(BF16)
    c64, s64, c128, s128 = c64_ref[...], s64_ref[...], c128_ref[...], s128_ref[...]
    lane = lax.broadcasted_iota(I32, c64.shape, 1)
    low_half = (lane & (A_HEAD_DIM // 2)) == 0

    def rope64(y):
        swapped = jnp.where(low_half, pltpu.roll(y, LANES - 32, 1), pltpu.roll(y, 32, 1))
        return y * c64 + swapped * s64

    def rope128(y):
        return y * c128 + pltpu.roll(y, 64, 1) * s128

    def proj(col, width):
        return _dot(hb, w_ref[:, col:col + width])

    def tiles(y):
        return [y[:, t * LANES:(t + 1) * LANES] for t in range(y.shape[1] // LANES)]

    def put_values_t(vt_ref, y):
        for t, v in enumerate(tiles(y)):
            vt_ref[t, 0:LANES, :] = v.T.astype(BF16)
            vt_ref[t, LANES:V_ROWS, :] = jnp.ones((V_ROWS - LANES, v.shape[0]), BF16)

    for t, y in enumerate(tiles(proj(COL_QA, 1024))):
        r = rope64(y)
        qa_ref[:, t * LANES:(t + 1) * LANES] = r.astype(BF16)
        qat_ref[t * LANES:(t + 1) * LANES, :] = (r * (A_HEAD_DIM ** -0.5 * LOG2E)).T.astype(BF16)
    for t, y in enumerate(tiles(proj(COL_KA, 1024))):
        r = rope64(y)
        kaf_ref[:, t * LANES:(t + 1) * LANES] = r
        kab_ref[:, t * LANES:(t + 1) * LANES] = r.astype(BF16)
    y = proj(COL_VA, 1024)
    vaf_ref[...] = y
    vab_ref[...] = y.astype(BF16)
    put_values_t(vat_ref, y)
    for t, y in enumerate(tiles(proj(COL_QB, 1024))):
        r = rope128(y)
        qb_ref[:, t * LANES:(t + 1) * LANES] = r.astype(BF16)
        qbt_ref[t * LANES:(t + 1) * LANES, :] = (r * (B_HEAD_DIM ** -0.5 * LOG2E)).T.astype(BF16)
    for t, y in enumerate(tiles(proj(COL_KB, 256))):
        r = rope128(y)
        kbf_ref[:, t * LANES:(t + 1) * LANES] = r
        kbb_ref[:, t * LANES:(t + 1) * LANES] = r.astype(BF16)
    y = proj(COL_VB, 256)
    vbf_ref[...] = y
    vbb_ref[...] = y.astype(BF16)
    put_values_t(vbt_ref, y)
    for t, y in enumerate(tiles(proj(COL_QI, 256))):
        r = rope64(y)
        qi_ref[:, t * LANES:(t + 1) * LANES] = r.astype(BF16)
        qit_ref[t * LANES:(t + 1) * LANES, :] = r.T.astype(BF16)
    y = proj(COL_KIWI, LANES)
    is_key = lane < IDX_DIM
    kiwi = jnp.where(is_key, rope64(y), y * (IDX_HEADS ** -0.5))
    kiwi_ref[...] = kiwi
    ki_ref[...] = kiwi[:, 0:IDX_DIM].astype(BF16)
    ki2 = jnp.where(is_key, kiwi, pltpu.roll(kiwi, IDX_DIM, 1)).astype(BF16)
    ki4_ref[:, 0:LANES] = ki2
    ki4_ref[:, LANES:2 * LANES] = ki2
    wt_ref[...] = kiwi.T[IDX_DIM:IDX_DIM + 8, :]


def _project(h, w1, tabs):
    n = h.shape[0]

    def row(width):
        return pl.BlockSpec((TILE_ROWS, width), lambda i: (i, 0))

    def col(height):
        return (pl.BlockSpec((height, TILE_ROWS), lambda i: (0, i)), (height, n))

    def vals(heads):
        return (pl.BlockSpec((heads, None, V_ROWS, TILE_ROWS), lambda i: (0, i, 0, 0)),
                (heads, n // TILE_ROWS, V_ROWS, TILE_ROWS))

    def rows(width):
        return (row(width), (n, width))

    outs = [rows(1024) + (BF16,), col(1024) + (BF16,), rows(1024) + (F32,), rows(1024) + (BF16,),
            rows(1024) + (F32,), rows(1024) + (BF16,), vals(A_HEADS) + (BF16,),
            rows(1024) + (BF16,), col(1024) + (BF16,), rows(256) + (F32,), rows(256) + (BF16,),
            rows(256) + (F32,), rows(256) + (BF16,), vals(B_KV_HEADS) + (BF16,),
            rows(256) + (BF16,), col(256) + (BF16,), rows(LANES) + (F32,), rows(IDX_DIM) + (BF16,),
            rows(256) + (BF16,), col(8) + (F32,)]
    return pl.pallas_call(
        _proj_kernel, grid=(n // TILE_ROWS,),
        in_specs=[row(D_MODEL), pl.BlockSpec((D_MODEL, N_PROJ), lambda i: (0, 0))] + [row(LANES)] * 4,
        out_specs=[spec for spec, _, _ in outs],
        out_shape=[jax.ShapeDtypeStruct(shape, dt) for _, shape, dt in outs],
        compiler_params=_params("parallel"), name="in_proj")(h, w1, *tabs)


def _subln(o1, o2, lam, g, lam_init):
    o = o1 - lam * o2
    y = o * lax.rsqrt(jnp.mean(o * o, -1, keepdims=True) + LN_EPS) * g
    return y * (1.0 - lam_init)


def _diff_p_kernel(lam_ref, qt_ref, k_ref, vt_ref, g_ref, o_ref, qq_sc, ta_sc, tb_sc, m_sc, acc_sc, *,
                   lam_init):
    tq = qt_ref.shape[1]
    i = pl.program_id(1)
    qt = qt_ref[...]
    row = lax.broadcasted_iota(I32, (LANES, 1), 0)
    zero = jnp.zeros_like(qt)
    qq_sc[...] = jnp.concatenate([jnp.where(row < A_HEAD_DIM, qt, zero),
                                  jnp.where(row >= A_HEAD_DIM, qt, zero)], axis=1)
    q_pos = i * tq + lax.broadcasted_iota(I32, (1, tq), 1)
    q_chunk = jnp.concatenate([q_pos, q_pos], axis=1) >> 6
    m_sc[...] = jnp.full(m_sc.shape, NEG_INF, F32)
    acc_sc[...] = jnp.zeros(acc_sc.shape, F32)
    n_kv = ((i + 1) * tq + TILE_K - 1) // TILE_K

    def scores(j):
        return _dot(k_ref[_tile_rows(j), :], qq_sc[...])

    def update(t_ref, j):
        _flash_update_t(t_ref, m_sc, acc_sc, [vt_ref[VT_PER_K * j + c] for c in range(VT_PER_K)], tq)

    def masked_step(j):
        tb_sc[...] = jnp.where(_chunk_mask_t(j, q_chunk), scores(j), NEG_INF)
        update(tb_sc, j)

    masked_step(0)
    n_pairs = jnp.maximum(n_kv - 2, 0) // 2

    @pl.when(n_pairs > 0)
    def _():
        ta_sc[...] = scores(1)

    def pair(r, carry):
        j = 1 + 2 * r
        tb_sc[...] = scores(j + 1)
        update(ta_sc, j)
        ta_sc[...] = scores(j + 2)
        update(tb_sc, j + 1)
        return carry

    lax.fori_loop(0, n_pairs, pair, 0)
    j_left = 1 + 2 * n_pairs

    @pl.when(j_left < n_kv - 1)
    def _():
        ta_sc[...] = scores(j_left)
        update(ta_sc, j_left)

    @pl.when(n_kv > 1)
    def _():
        masked_step(n_kv - 1)

    acc = acc_sc[...]
    outs = [acc[0:LANES, mp * tq:(mp + 1) * tq] / acc[LANES:LANES + 1, mp * tq:(mp + 1) * tq] for mp in range(2)]
    o = outs[0] - lam_ref[0] * outs[1]
    y = o * lax.rsqrt(jnp.mean(o * o, axis=0, keepdims=True) + LN_EPS) * g_ref[...] * (1.0 - lam_init)
    o_ref[...] = y.T.astype(BF16)


def _diff_attn_prompt(lam, qat, ka, vat, g_col, lp, lam_init):
    n = ka.shape[0]
    return pl.pallas_call(
        functools.partial(_diff_p_kernel, lam_init=lam_init),
        grid=(A_HEADS, lp // TILE_Q_A),
        in_specs=[pl.BlockSpec(memory_space=pltpu.SMEM),
                  pl.BlockSpec((LANES, TILE_Q_A), lambda h, i: (h, i)),
                  pl.BlockSpec((lp, LANES), lambda h, i: (0, h)),
                  pl.BlockSpec((None, lp // TILE_ROWS, V_ROWS, TILE_ROWS), lambda h, i: (h, 0, 0, 0)),
                  pl.BlockSpec((LANES, 1), lambda h, i: (0, 0))],
        out_specs=pl.BlockSpec((TILE_Q_A, LANES), lambda h, i: (i, h)),
        out_shape=jax.ShapeDtypeStruct((n, A_HEADS * LANES), BF16),
        scratch_shapes=[pltpu.VMEM((LANES, 2 * TILE_Q_A), BF16),
                        pltpu.VMEM((TILE_K, 2 * TILE_Q_A), F32), pltpu.VMEM((TILE_K, 2 * TILE_Q_A), F32),
                        pltpu.VMEM((1, 2 * TILE_Q_A), F32), pltpu.VMEM((V_ROWS, 2 * TILE_Q_A), F32)],
        compiler_params=_params("parallel", "parallel"), name="diff_attn_prompt")(lam, qat, ka, vat, g_col)


def _diff_s_kernel(lam_ref, q_ref, ck_ref, cv_ref, kn_ref, vn_ref, g_ref, prev_ref, o_ref, *, lam_init, n_keys):
    del prev_ref
    ds_ = q_ref.shape[0]
    past = ck_ref.shape[0]
    pad = n_keys - past - ds_
    lane = lax.broadcasted_iota(I32, (1, LANES), 1)
    valid = lax.broadcasted_iota(I32, (1, n_keys), 1) < past + ds_
    for h in range(A_HEADS):
        cols = slice(h * LANES, (h + 1) * LANES)
        q = q_ref[:, cols]
        zero = jnp.zeros_like(q)
        k = jnp.concatenate([ck_ref[:, cols].astype(BF16), kn_ref[:, cols], jnp.zeros((pad, LANES), BF16)], 0)
        v = jnp.concatenate([cv_ref[:, cols].astype(BF16), vn_ref[:, cols], jnp.zeros((pad, LANES), BF16)], 0)
        outs = []
        for qz in (jnp.where(lane < A_HEAD_DIM, q, zero), jnp.where(lane >= A_HEAD_DIM, q, zero)):
            s = jnp.where(valid, _dot_nt(qz, k) * (A_HEAD_DIM ** -0.5), NEG_INF)
            p = jnp.exp(s - jnp.max(s, axis=1, keepdims=True))
            outs.append(_dot(p.astype(BF16), v) / jnp.sum(p, axis=1, keepdims=True))
        o_ref[:, cols] = _subln(outs[0], outs[1], lam_ref[0], g_ref[...], lam_init).astype(BF16)


def _diff_attn_sample(lam, qa, cache_k, cache_v, ka, va, g, ya, layer, lp, lam_init):
    _, nb, past, width = cache_k.shape
    n = qa.shape[0]
    ds_ = (n - lp) // nb
    n_keys = -(-(past + ds_) // LANES) * LANES
    base = lp // ds_
    new = pl.BlockSpec((ds_, width), lambda b: (base + b, 0))
    cache = pl.BlockSpec((None, None, past, width), lambda b: (layer, b, 0, 0))
    return pl.pallas_call(
        functools.partial(_diff_s_kernel, lam_init=lam_init, n_keys=n_keys), grid=(nb,),
        in_specs=[pl.BlockSpec(memory_space=pltpu.SMEM), new, cache, cache, new, new,
                  pl.BlockSpec((1, LANES), lambda b: (0, 0)), pl.BlockSpec(memory_space=pl.ANY)],
        out_specs=new, out_shape=jax.ShapeDtypeStruct(ya.shape, ya.dtype),
        input_output_aliases={7: 0},
        compiler_params=_params("parallel"), name="diff_attn_sample")(lam, qa, cache_k, cache_v, ka, va, g, ya)


def _index_scores(qiz, w_cols, ki4, rows):
    r = _dot_nt(qiz, ki4)
    sc = w_cols[0] * jnp.maximum(r[0:rows], 0.0)
    for h in range(1, IDX_HEADS):
        sc = sc + w_cols[h] * jnp.maximum(r[h * rows:(h + 1) * rows], 0.0)
    return sc


def _stack_idx_queries(qi, rows):
    lane = lax.broadcasted_iota(I32, (1, IDX_HEADS * IDX_DIM), 1)
    zero = jnp.zeros_like(qi)
    return [jnp.where((lane >= IDX_DIM * h) & (lane < IDX_DIM * (h + 1)), qi, zero) for h in range(IDX_HEADS)]


def _selected(key, t, need, seen, u):
    eq = key == t
    prefix = _dot(jnp.where(eq, 1.0, 0.0).astype(BF16), u)
    sel = ((key > t) | (eq & (seen + prefix <= need))) & (key > KEY_HALF_NEG)
    return sel, seen + prefix[:, -1:]


def _dsa_p_kernel(qbt_ref, qit_ref, wt_ref, kb_ref, vbt_ref, ki_ref, tri_ref, o_ref,
                  key_sc, qq_sc, ta_sc, tb_sc, m_sc, acc_sc, *, n_sel):
    tq = qbt_ref.shape[1]
    i = pl.program_id(0)
    n_kv = ((i + 1) * tq + TILE_K - 1) // TILE_K
    w = wt_ref[...] * (IDX_DIM ** -0.5)
    w_rows = [w[h:h + 1] for h in range(IDX_HEADS)]
    q_chunk = (i * tq + lax.broadcasted_iota(I32, (1, tq), 1)) >> 6

    def score_tile(j, carry):
        ki = ki_ref[_tile_rows(j), :]
        sc = None
        for h in range(IDX_HEADS):
            r = w_rows[h] * jnp.maximum(_dot(ki, qit_ref[h * IDX_DIM:(h + 1) * IDX_DIM, :]), 0.0)
            sc = r if sc is None else sc + r
        key_sc[j] = _sort_key(jnp.where(_chunk_mask_t(j, q_chunk), sc, NEG_INF))
        return carry

    lax.fori_loop(0, n_kv, score_tile, 0)

    def count_ge(c):
        def body(j, acc):
            hit = jnp.where(key_sc[j] >= c, 1.0, 0.0)
            rows = TILE_K
            while rows > COUNT_ROWS:
                rows //= 2
                hit = hit[0:rows] + hit[rows:2 * rows]
            return acc + hit
        return jnp.sum(lax.fori_loop(0, n_kv, body, jnp.zeros((COUNT_ROWS, tq), F32)), axis=0, keepdims=True)

    thr = _kth_largest_key(count_ge, (1, tq), float(n_sel))
    need = float(n_sel) - count_ge(thr + 1)
    need = jnp.where(thr > KEY_HALF_NEG, need, 0.0)
    floor_key = jnp.maximum(thr, KEY_HALF_NEG)

    for c in range(B_KV_HEADS):
        for g in range(B_GROUP):
            hh = c * B_GROUP + g
            qq_sc[c, :, g * tq:(g + 1) * tq] = qbt_ref[hh * LANES:(hh + 1) * LANES, :]
    m_sc[...] = jnp.full(m_sc.shape, NEG_INF, F32)
    acc_sc[...] = jnp.zeros(acc_sc.shape, F32)

    def scores(j, c):
        return _dot(kb_ref[_tile_rows(j), c * LANES:(c + 1) * LANES], qq_sc[c])

    def update(t_ref, j, c, sel):
        _flash_update_t(t_ref, m_sc.at[c], acc_sc.at[c],
                        [vbt_ref[c, VT_PER_K * j + cc] for cc in range(VT_PER_K)], tq, sel)

    ta_sc[...] = scores(0, 0)

    def attend(j, seen):
        key = key_sc[j]
        tie = key == thr
        prefix = _dot(tri_ref[...], jnp.where(tie, 1.0, 0.0).astype(BF16))
        sel = (key > floor_key) | (tie & (seen + prefix <= need))
        tb_sc[...] = scores(j, 1)
        update(ta_sc, j, 0, sel)
        ta_sc[...] = scores(jnp.minimum(j + 1, n_kv - 1), 0)
        update(tb_sc, j, 1, sel)
        return seen + prefix[TILE_K - 1:TILE_K, :]

    lax.fori_loop(0, n_kv, attend, jnp.zeros((1, tq), F32))
    for c in range(B_KV_HEADS):
        acc = acc_sc[c]
        for g in range(B_GROUP):
            hh = c * B_GROUP + g
            a = acc[:, g * tq:(g + 1) * tq]
            o_ref[:, hh * LANES:(hh + 1) * LANES] = (a[0:LANES] / a[LANES:LANES + 1]).T.astype(BF16)


def _dsa_prompt(qbt, qit, wt, kb, vbt, ki, tri, lp, n_sel):
    n = kb.shape[0]
    tq = TILE_Q_B

    def col(height):
        return pl.BlockSpec((height, tq), lambda i: (0, i))

    def resident(shape):
        return pl.BlockSpec(shape, lambda i: (0,) * len(shape), pipeline_mode=pl.Buffered(1))

    return pl.pallas_call(
        functools.partial(_dsa_p_kernel, n_sel=n_sel), grid=(lp // tq,),
        in_specs=[col(1024), col(256), col(8), resident((lp, 256)),
                  resident((B_KV_HEADS, lp // TILE_ROWS, V_ROWS, TILE_ROWS)), resident((lp, IDX_DIM)),
                  resident((TILE_K, TILE_K))],
        out_specs=pl.BlockSpec((tq, 1024), lambda i: (i, 0)),
        out_shape=jax.ShapeDtypeStruct((n, 1024), BF16),
        scratch_shapes=[pltpu.VMEM((lp // TILE_K, TILE_K, tq), I32),
                        pltpu.VMEM((B_KV_HEADS, LANES, B_GROUP * tq), BF16),
                        pltpu.VMEM((TILE_K, B_GROUP * tq), F32), pltpu.VMEM((TILE_K, B_GROUP * tq), F32),
                        pltpu.VMEM((B_KV_HEADS, 1, B_GROUP * tq), F32),
                        pltpu.VMEM((B_KV_HEADS, V_ROWS, B_GROUP * tq), F32)],
        compiler_params=_params("parallel"), name="dsa_prompt")(qbt, qit, wt, kb, vbt, ki, tri)


def _dsa_s_kernel(qb_ref, qi_ref, kiwi_ref, ckb_ref, cvb_ref, cki_ref, kbn_ref, vbn_ref, ki4n_ref,
                  rep_ref, u_ref, prev_ref, o_ref, *, n_sel, n_keys):
    del prev_ref
    tq = qb_ref.shape[0]
    past = ckb_ref.shape[0]
    pad = n_keys - past - tq
    width = B_KV_HEADS * B_HEAD_DIM
    kb = jnp.concatenate([ckb_ref[...].astype(BF16), kbn_ref[...], jnp.zeros((pad, width), BF16)], 0)
    vb = jnp.concatenate([cvb_ref[...].astype(BF16), vbn_ref[...], jnp.zeros((pad, width), BF16)], 0)
    ki4_past = _dot(cki_ref[...].astype(BF16), rep_ref[...]).astype(BF16)
    ki4 = jnp.concatenate([ki4_past, ki4n_ref[...], jnp.zeros((pad, width), BF16)], 0)
    qiz = jnp.concatenate(_stack_idx_queries(qi_ref[...], tq), 0)
    wq = kiwi_ref[...] * (IDX_DIM ** -0.5)
    w_cols = [wq[:, IDX_DIM + h:IDX_DIM + h + 1] for h in range(IDX_HEADS)]
    sc = _index_scores(qiz, w_cols, ki4, tq)
    valid = lax.broadcasted_iota(I32, (1, n_keys), 1) < past + tq
    key = _sort_key(jnp.where(valid, sc, NEG_INF))

    def count_ge(c):
        return jnp.sum(jnp.where(key >= c, 1.0, 0.0), axis=1, keepdims=True)

    t = _kth_largest_key(count_ge, (tq, 1), float(n_sel))
    need = float(n_sel) - count_ge(t + 1)
    sel, _ = _selected(key, t, need, jnp.zeros((tq, 1), F32), u_ref[...])
    for c in range(B_KV_HEADS):
        cols = slice(c * LANES, (c + 1) * LANES)
        q = jnp.concatenate([qb_ref[:, (c * B_GROUP + g) * LANES:(c * B_GROUP + g + 1) * LANES]
                             for g in range(B_GROUP)], 0)
        s_all = _dot_nt(q, kb[:, cols]) * (B_HEAD_DIM ** -0.5)
        for g in range(B_GROUP):
            hh = c * B_GROUP + g
            s = jnp.where(sel, s_all[g * tq:(g + 1) * tq], NEG_INF)
            p = jnp.exp(s - jnp.max(s, axis=1, keepdims=True))
            o = _dot(p.astype(BF16), vb[:, cols]) / jnp.sum(p, axis=1, keepdims=True)
            o_ref[:, hh * LANES:(hh + 1) * LANES] = o.astype(BF16)


def _dsa_sample(qb, qi, kiwi, cache_k, cache_v, cache_idx, kb, vb, ki4, rep, u, yb, layer, lp, n_sel):
    _, nb, past, _ = cache_k.shape
    n = qb.shape[0]
    ds_ = (n - lp) // nb
    n_keys = u.shape[0]
    base = lp // ds_

    def new(width):
        return pl.BlockSpec((ds_, width), lambda b: (base + b, 0))

    def cache(width):
        return pl.BlockSpec((None, None, past, width), lambda b: (layer, b, 0, 0))

    def const(shape):
        return pl.BlockSpec(shape, lambda b: (0, 0))

    return pl.pallas_call(
        functools.partial(_dsa_s_kernel, n_sel=n_sel, n_keys=n_keys), grid=(nb,),
        in_specs=[new(1024), new(256), new(LANES), cache(256), cache(256), cache(IDX_DIM),
                  new(256), new(256), new(256), const(rep.shape), const(u.shape),
                  pl.BlockSpec(memory_space=pl.ANY)],
        out_specs=new(1024), out_shape=jax.ShapeDtypeStruct(yb.shape, yb.dtype),
        input_output_aliases={11: 0},
        compiler_params=_params("parallel"), name="dsa_sample")(
            qb, qi, kiwi, cache_k, cache_v, cache_idx, kb, vb, ki4, rep, u, yb)


def _merge_kernel(h_ref, ya_ref, yb_ref, wga_ref, wgb_ref, wpa_ref, wpb_ref, wo_ref, g_ref, b_ref,
                  wr_ref, bias_ref, o_ref, gates_ref):
    h = h_ref[...]
    hb = h.astype(BF16)
    mixed = (jax.nn.sigmoid(_dot(hb, wga_ref[...])) * _dot(ya_ref[...], wpa_ref[...])
             + jax.nn.sigmoid(_dot(hb, wgb_ref[...])) * _dot(yb_ref[...], wpb_ref[...]))
    y = _dot(mixed.astype(BF16), wo_ref[...])
    h1 = _layernorm(DEEPNORM_ALPHA * h + y, g_ref[...], b_ref[...])
    o_ref[...] = h1
    gates_ref[...] = _route(jax.nn.sigmoid(_dot(h1.astype(BF16), wr_ref[...])), bias_ref[...])


def _merge(h, ya, yb, wga, wgb, wpa, wpb, wo, g, b, wr, bias):
    n, d = h.shape
    row = pl.BlockSpec((TILE_ROWS, d), lambda i: (i, 0))
    mat = pl.BlockSpec((d, d), lambda i: (0, 0))
    vec = pl.BlockSpec((1, d), lambda i: (0, 0))
    return pl.pallas_call(
        _merge_kernel, grid=(n // TILE_ROWS,),
        in_specs=[row, row, row] + [mat] * 5 + [vec, vec, pl.BlockSpec((d, N_EXPERTS), lambda i: (0, 0)),
                                                 pl.BlockSpec((1, N_EXPERTS), lambda i: (0, 0))],
        out_specs=[row, pl.BlockSpec((TILE_ROWS, N_EXPERTS), lambda i: (i, 0))],
        out_shape=[jax.ShapeDtypeStruct((n, d), F32), jax.ShapeDtypeStruct((n, N_EXPERTS), F32)],
        compiler_params=_params("parallel"), name="merge_ln1_route")(
            h, ya, yb, wga, wgb, wpa, wpb, wo, g.reshape(1, d), b.reshape(1, d), wr,
            bias.reshape(1, N_EXPERTS))


def _route(s, bias):
    sel = s + bias
    lane = lax.broadcasted_iota(I32, (1, N_EXPERTS), 1)
    lane_f = lane.astype(F32)
    group = lane >> 3
    neg = -jnp.inf

    def first_argmax(x, m):
        return jnp.min(jnp.where(x == m, lane_f, float(N_EXPERTS)), axis=1, keepdims=True)

    g_score = []
    for g in range(N_GROUPS):
        x = jnp.where(group == g, sel, neg)
        m1 = jnp.max(x, axis=1, keepdims=True)
        x = jnp.where(lane_f == first_argmax(x, m1), neg, x)
        g_score.append(m1 + jnp.max(x, axis=1, keepdims=True))
    e_keep = None
    for g in range(N_GROUPS):
        ahead = jnp.zeros_like(g_score[g])
        for o in range(N_GROUPS):
            if o != g:
                beats = (g_score[o] > g_score[g]) | ((g_score[o] == g_score[g]) & (o < g))
                ahead = ahead + jnp.where(beats, 1.0, 0.0)
        keep = (group == g) & (ahead < float(TOPK_GROUPS))
        e_keep = keep if e_keep is None else (e_keep | keep)
    x = jnp.where(e_keep, sel, NEG_INF)
    chosen = None
    for _ in range(TOP_K):
        hit = lane_f == first_argmax(x, jnp.max(x, axis=1, keepdims=True))
        chosen = hit if chosen is None else (chosen | hit)
        x = jnp.where(hit, neg, x)
    w = jnp.where(chosen, s, 0.0)
    return w / jnp.sum(w, axis=1, keepdims=True) * ROUTE_SCALE


def _silu(x):
    return x * jax.nn.sigmoid(x)


def _moe_kernel(h_ref, gates_ref, wg_ref, wu_ref, wd_ref, sg_ref, su_ref, sd_ref, g_ref, b_ref, o_ref,
                xb_sc, acc_sc):
    step = pl.program_id(1)

    @pl.when(step == 0)
    def _():
        xb = h_ref[...].astype(BF16)
        xb_sc[...] = xb
        a = _silu(_dot(xb, sg_ref[...])) * _dot(xb, su_ref[...])
        acc_sc[...] = _dot(a.astype(BF16), sd_ref[...])

    xb = xb_sc[...]
    gates = gates_ref[...]
    lane = lax.broadcasted_iota(I32, (1, N_EXPERTS), 1)
    acts = []
    for e in range(EXPERT_BLOCK):
        gate = jnp.sum(jnp.where(lane == step * EXPERT_BLOCK + e, gates, 0.0), axis=1, keepdims=True)
        a = _silu(_dot(xb, wg_ref[e])) * _dot(xb, wu_ref[e]) * gate
        acts.append(a.astype(BF16))
    wd = wd_ref[...]
    acc_sc[...] += _dot(jnp.concatenate(acts, axis=1), wd.reshape(wd.shape[0] * wd.shape[1], wd.shape[2]))

    @pl.when(step == pl.num_programs(1) - 1)
    def _():
        o_ref[...] = _layernorm(DEEPNORM_ALPHA * h_ref[...] + acc_sc[...], g_ref[...], b_ref[...])


def _moe(h, gates, wg, wu, wd, sg, su, sd, g, b):
    n, d = h.shape
    f = wg.shape[-1]
    row = pl.BlockSpec((TILE_MOE, d), lambda r, e: (r, 0))
    vec = pl.BlockSpec((1, d), lambda r, e: (0, 0))
    return pl.pallas_call(
        _moe_kernel, grid=(n // TILE_MOE, N_EXPERTS // EXPERT_BLOCK),
        in_specs=[row, pl.BlockSpec((TILE_MOE, N_EXPERTS), lambda r, e: (r, 0)),
                  pl.BlockSpec((EXPERT_BLOCK, d, f), lambda r, e: (e, 0, 0)),
                  pl.BlockSpec((EXPERT_BLOCK, d, f), lambda r, e: (e, 0, 0)),
                  pl.BlockSpec((EXPERT_BLOCK, f, d), lambda r, e: (e, 0, 0)),
                  pl.BlockSpec((d, sg.shape[1]), lambda r, e: (0, 0)),
                  pl.BlockSpec((d, sg.shape[1]), lambda r, e: (0, 0)),
                  pl.BlockSpec((sg.shape[1], d), lambda r, e: (0, 0)), vec, vec],
        out_specs=row, out_shape=jax.ShapeDtypeStruct((n, d), F32),
        scratch_shapes=[pltpu.VMEM((TILE_MOE, d), BF16), pltpu.VMEM((TILE_MOE, d), F32)],
        compiler_params=_params("parallel", "arbitrary"), name="moe_ln2")(
            h, gates, wg, wu, wd, sg, su, sd, g.reshape(1, d), b.reshape(1, d))


def _rope_tables(pos):
    def table(half, reps):
        inv = ROPE_THETA ** (-jnp.arange(half, dtype=F32) / half)
        ang = pos.astype(F32)[:, None] * inv[None, :]
        cos, sin = jnp.cos(ang), jnp.sin(ang)
        return (jnp.tile(jnp.concatenate([cos, cos], -1), (1, reps)),
                jnp.tile(jnp.concatenate([-sin, sin], -1), (1, reps)))
    c64, s64 = table(A_HEAD_DIM // 2, 2)
    c128, s128 = table(B_HEAD_DIM // 2, 1)
    return c64, s64, c128, s128


def kernel(x_prompt, x_sample, cache_a_k, cache_a_v, cache_b_k, cache_b_v, cache_b_idx, meta_tokens, ln_in_g, ln_in_b, w_in, lam_q1, lam_k1, lam_q2, lam_k2, subln_g, w_proj_a, w_proj_b, w_out, ln1_g, ln1_b, w_router, router_bias, w_expert_gate, w_expert_up, w_expert_down, w_shared_gate, w_shared_up, w_shared_down, ln2_g, ln2_b):
    batch, seq, d = x_prompt.shape
    nb, ds_, _ = x_sample.shape
    depth, _, past = cache_a_k.shape[:3]
    assert batch == 1 and d == D_MODEL and seq % TILE_K == 0 and ds_ == CHUNK
    lp = FRONT_PAD + N_META + seq
    ns = nb * ds_
    n = lp + ns
    assert n % TILE_MOE == 0 and past % 16 == 0
    n_sel_p = min(TOPK_MAX, seq // 4)
    n_sel_s = min(TOPK_MAX, (past + ds_) // 4)
    n_keys_s = -(-(past + ds_) // LANES) * LANES

    x_all = jnp.concatenate([jnp.zeros((FRONT_PAD, d), F32), meta_tokens.astype(F32), x_prompt[0],
                             x_sample.reshape(ns, d)], 0)
    pos = jnp.concatenate([jnp.maximum(jnp.arange(lp, dtype=jnp.int32) - FRONT_PAD, 0),
                           jnp.tile(past + jnp.arange(ds_, dtype=jnp.int32), nb)])
    tabs = _rope_tables(pos)
    tri_p = jnp.tril(jnp.ones((TILE_K, TILE_K), BF16))
    u_s = jnp.triu(jnp.ones((n_keys_s, n_keys_s), BF16))
    rep = jnp.tile(jnp.eye(IDX_DIM, dtype=BF16), (1, IDX_HEADS))
    ck_a = cache_a_k.reshape(depth, nb, past, -1)
    cv_a = cache_a_v.reshape(depth, nb, past, -1)
    ck_b = cache_b_k.reshape(depth, nb, past, -1)
    cv_b = cache_b_v.reshape(depth, nb, past, -1)

    h = _ln_rows(x_all, ln_in_g, ln_in_b)
    rows = []
    for l in range(depth):
        lam_init = 0.8 - 0.6 * math.exp(-0.3 * l)
        lam = (jnp.exp(jnp.sum(lam_q1[l].astype(F32) * lam_k1[l].astype(F32)))
               - jnp.exp(jnp.sum(lam_q2[l].astype(F32) * lam_k2[l].astype(F32))) + lam_init).reshape(1)
        w1 = jnp.concatenate([w_in[l][:, :N_RAW_PROJ], jnp.zeros((d, N_PROJ - N_RAW_PROJ), F32)], 1).astype(BF16)
        wga = w_in[l][:, N_RAW_PROJ:N_RAW_PROJ + d].astype(BF16)
        wgb = w_in[l][:, N_RAW_PROJ + d:].astype(BF16)
        (qa, qat, ka_f, ka, va_f, va, vat, qb, qbt, kb_f, kb, vb_f, vb, vbt, qi, qit, kiwi, ki, ki4,
         wt) = _project(h, w1, tabs)
        rows.append((ka_f, va_f, kb_f, vb_f, kiwi))
        g = subln_g[l].astype(F32)
        ya = _diff_attn_prompt(lam, qat, ka, vat, g.reshape(LANES, 1), lp, lam_init)
        ya = _diff_attn_sample(lam, qa, ck_a, cv_a, ka, va, g.reshape(1, LANES), ya, l, lp, lam_init)
        yb = _dsa_prompt(qbt, qit, wt, kb, vbt, ki, tri_p, lp, n_sel_p)
        yb = _dsa_sample(qb, qi, kiwi, ck_b, cv_b, cache_b_idx, kb, vb, ki4, rep, u_s, yb, l, lp, n_sel_s)
        h, gates = _merge(h, ya, yb, wga, wgb, w_proj_a[l].astype(BF16), w_proj_b[l].astype(BF16),
                          w_out[l].astype(BF16), ln1_g[l], ln1_b[l], w_router[l].astype(BF16),
                          router_bias[l].astype(F32))
        h = _moe(h, gates, w_expert_gate[l].astype(BF16), w_expert_up[l].astype(BF16),
                 w_expert_down[l].astype(BF16), w_shared_gate[l].astype(BF16), w_shared_up[l].astype(BF16),
                 w_shared_down[l].astype(BF16), ln2_g[l], ln2_b[l])

    p0 = FRONT_PAD
    y_prompt = h[p0 + N_META:lp].reshape(1, seq, d)
    y_sample = h[lp:].reshape(nb, ds_, d)

    def stack(idx, width, tail):
        full = jnp.stack([r[idx][:, :width] for r in rows])
        return (full[:, p0:lp].reshape((depth, 1, lp - p0) + tail),
                full[:, lp:].reshape((depth, nb, ds_) + tail))

    ak_p, ak_s = stack(0, 1024, (2 * A_HEADS, A_HEAD_DIM))
    av_p, av_s = stack(1, 1024, (A_HEADS, 2 * A_HEAD_DIM))
    bk_p, bk_s = stack(2, 256, (B_KV_HEADS, B_HEAD_DIM))
    bv_p, bv_s = stack(3, 256, (B_KV_HEADS, B_HEAD_DIM))
    bi_p, bi_s = stack(4, IDX_DIM, (IDX_DIM,))
    return (y_prompt, y_sample, ak_p, av_p, bk_p, bv_p, bi_p, ak_s, av_s, bk_s, bv_s, bi_s)
```

```python
import functools
import math

import numpy as np
import jax
import jax.numpy as jnp
from jax import lax
from jax.experimental import pallas as pl
from jax.experimental.pallas import tpu as pltpu

F32 = jnp.float32
BF16 = jnp.bfloat16
I32 = jnp.int32

MODEL_DEPTH = 4
CHUNK = 64
N_META = 16
ROPE_THETA = 10000.0
LN_EPS = 1e-5
NEG_INF = -1e30
DEEPNORM_ALPHA = (2 * MODEL_DEPTH) ** 0.25
A_HEADS = 8
A_HEAD_DIM = 64
B_HEADS = 8
B_KV_HEADS = 2
B_GROUP = B_HEADS // B_KV_HEADS
B_HEAD_DIM = 128
IDX_HEADS = 4
IDX_DIM = 64
TOPK_MAX = 256
N_EXPERTS = 64
TOP_K = 8
N_GROUPS = 8
GROUP_SIZE = N_EXPERTS // N_GROUPS
TOPK_GROUPS = 4
ROUTE_SCALE = 2.5

LANES = 128
D_MODEL = 1024
N_PROJ = 4992
COL_QA, COL_KA, COL_VA, COL_QB, COL_KB, COL_VB, COL_QI, COL_KIWI = (
    0, 1024, 2048, 3072, 4096, 4352, 4608, 4864)
N_RAW_PROJ = 4932

TILE_K = 512
FRONT_PAD = TILE_K - N_META
TILE_Q_A = 256
TILE_Q_B = 256
TILE_ROWS = 256
TILE_MOE = 512
EXPERT_BLOCK = 4
VMEM_LIMIT = 56 * 1024 * 1024
BIG_CHUNK = 1 << 30
INT_MIN = -(2 ** 31)
LOG2E = 1.4426950408889634
V_ROWS = 144
VT_PER_K = TILE_K // TILE_ROWS
COUNT_ROWS = 16
COUNT_CHAINS = 4
SOFTMAX_ROWS = 128


def _f32_key(v):
    b = int(np.float32(v).view(np.int32))
    return b ^ ((b >> 31) & 0x7FFFFFFF)


KEY_HALF_NEG = _f32_key(0.5 * NEG_INF)


def _dot(a, b):
    return jnp.dot(a, b, preferred_element_type=F32)


def _dot_nt(a, b):
    return lax.dot_general(a, b, (((1,), (1,)), ((), ())), preferred_element_type=F32)


def _layernorm(x, g, b):
    mu = jnp.mean(x, -1, keepdims=True)
    xc = x - mu
    var = jnp.mean(xc * xc, -1, keepdims=True)
    return xc * lax.rsqrt(var + LN_EPS) * g + b


def _sort_key(x):
    bits = lax.bitcast_convert_type(x, I32)
    return bits ^ ((bits >> 31) & 0x7FFFFFFF)


def _kth_largest_key(count_ge, shape, k):
    def body(b, tu):
        cand = tu | jnp.left_shift(jnp.int32(1), 31 - b)
        c = count_ge(cand ^ INT_MIN)
        return jnp.where(c >= k, cand, tu)
    tu = lax.fori_loop(0, 32, body, jnp.zeros(shape, I32))
    return tu ^ INT_MIN


def _kth_largest_key_bracketed(count_ge, top_key, k):
    def flags(b):
        return jnp.where(b, 1.0, 0.0)

    def any_set(f):
        return jnp.max(f) > 0.0

    enough = count_ge(jnp.full_like(top_key, KEY_HALF_NEG + 1)) >= k
    top_tie = enough & (count_ge(top_key) >= k)
    lo = jnp.where(top_tie, top_key, INT_MIN)
    hi = top_key
    done = top_tie | ~enough

    def gallop_cond(s):
        return (s[0] < 5) & any_set(s[3])

    def gallop(s):
        g, lo, hi, want = s
        need_lo = want > 0.5
        step = jnp.left_shift(jnp.int32(1 << 23), 2 * g)
        cand = jnp.where((g < 4) & (top_key >= INT_MIN + step), top_key - step, INT_MIN)
        ok = count_ge(cand) >= k
        lo = jnp.where(need_lo & ok, cand, lo)
        hi = jnp.where(need_lo & ~ok, cand, hi)
        return g + 1, lo, hi, flags(need_lo & ~ok)

    _, lo, hi, _ = lax.while_loop(gallop_cond, gallop, (jnp.int32(0), lo, hi, flags(~done)))

    def bisect_cond(s):
        return any_set(s[2])

    def bisect(s):
        lo, hi, open_f = s
        open_ = open_f > 0.5
        mid = lo + lax.shift_right_logical(hi - lo, jnp.full_like(lo, 1))
        c = count_ge(mid)
        ge = c >= k
        lo = jnp.where(open_ & ge, mid, lo)
        hi = jnp.where(open_ & ~ge, mid, hi)
        return lo, hi, flags(open_ & (c != k) & (hi - lo != 1))

    lo, _, _ = lax.while_loop(bisect_cond, bisect, (lo, hi, flags(~done & (hi - lo != 1))))
    return lo


def _tile_rows(j):
    return pl.ds(pl.multiple_of(j * TILE_K, TILE_K), TILE_K)


def _chunk_mask_t(j, q_chunk):
    k_idx = j * TILE_K + lax.broadcasted_iota(I32, (TILE_K, 1), 0)
    return jnp.where(k_idx >= FRONT_PAD, k_idx >> 6, BIG_CHUNK) <= q_chunk


def _flash_update_t(t_ref, m_ref, acc_ref, vt_tiles, width):
    chunks = [slice(r, r + SOFTMAX_ROWS) for r in range(0, TILE_K, SOFTMAX_ROWS)]
    for b in range(t_ref.shape[1] // width):
        cols = slice(b * width, (b + 1) * width)
        m_prev = m_ref[:, cols]
        m_new = m_prev
        for rows in chunks:
            m_new = jnp.maximum(m_new, jnp.max(t_ref[rows, cols], axis=0, keepdims=True))
        p = jnp.concatenate([jnp.exp2(t_ref[rows, cols] - m_new).astype(BF16) for rows in chunks], axis=0)
        pv = _dot(vt_tiles[0], p[0:TILE_ROWS])
        for c in range(1, VT_PER_K):
            pv = pv + _dot(vt_tiles[c], p[c * TILE_ROWS:(c + 1) * TILE_ROWS])
        acc_ref[:, cols] = jnp.exp2(m_prev - m_new) * acc_ref[:, cols] + pv
        m_ref[:, cols] = m_new


def _params(*sem):
    return pltpu.CompilerParams(dimension_semantics=sem, vmem_limit_bytes=VMEM_LIMIT)


def _ln_kernel(x_ref, g_ref, b_ref, o_ref):
    o_ref[...] = _layernorm(x_ref[...], g_ref[...], b_ref[...])


def _ln_rows(x, g, b):
    n, d = x.shape
    row = pl.BlockSpec((TILE_ROWS, d), lambda i: (i, 0))
    vec = pl.BlockSpec((1, d), lambda i: (0, 0))
    return pl.pallas_call(
        _ln_kernel, grid=(n // TILE_ROWS,), in_specs=[row, vec, vec], out_specs=row,
        out_shape=jax.ShapeDtypeStruct((n, d), F32), compiler_params=_params("parallel"),
        name="ln_in")(x, g.reshape(1, d), b.reshape(1, d))


def _proj_kernel(h_ref, w_ref, c64_ref, s64_ref, c128_ref, s128_ref,
                 qa_ref, qat_ref, kaf_ref, kab_ref, vaf_ref, vab_ref, vat_ref, qb_ref, qbt_ref,
                 kbf_ref, kbb_ref, vbf_ref, vbb_ref, vbt_ref, qi_ref, qit_ref, kiwi_ref, ki_ref,
                 ki4_ref, wt_ref):
    hb = h_ref[...].astype(BF16)
    c64, s64, c128, s128 = c64_ref[...], s64_ref[...], c128_ref[...], s128_ref[...]
    lane = lax.broadcasted_iota(I32, c64.shape, 1)
    low_half = (lane & (A_HEAD_DIM // 2)) == 0

    def rope64(y):
        swapped = jnp.where(low_half, pltpu.roll(y, LANES - 32, 1), pltpu.roll(y, 32, 1))
        return y * c64 + swapped * s64

    def rope128(y):
        return y * c128 + pltpu.roll(y, 64, 1) * s128

    def proj(col, width):
        return _dot(hb, w_ref[:, col:col + width])

    def tiles(y):
        return [y[:, t * LANES:(t + 1) * LANES] for t in range(y.shape[1] // LANES)]

    def put_values_t(vt_ref, y):
        for t, v in enumerate(tiles(y)):
            vt_ref[t, 0:LANES, :] = v.T.astype(BF16)
            vt_ref[t, LANES:V_ROWS, :] = jnp.ones((V_ROWS - LANES, v.shape[0]), BF16)

    for t, y in enumerate(tiles(proj(COL_QA, 1024))):
        r = rope64(y)
        qa_ref[:, t * LANES:(t + 1) * LANES] = r.astype(BF16)
        qat_ref[t * LANES:(t + 1) * LANES, :] = (r * (A_HEAD_DIM ** -0.5 * LOG2E)).T.astype(BF16)
    for t, y in enumerate(tiles(proj(COL_KA, 1024))):
        r = rope64(y)
        kaf_ref[:, t * LANES:(t + 1) * LANES] = r
        kab_ref[:, t * LANES:(t + 1) * LANES] = r.astype(BF16)
    y = proj(COL_VA, 1024)
    vaf_ref[...] = y
    vab_ref[...] = y.astype(BF16)
    put_values_t(vat_ref, y)
    for t, y in enumerate(tiles(proj(COL_QB, 1024))):
        r = rope128(y)
        qb_ref[:, t * LANES:(t + 1) * LANES] = r.astype(BF16)
        qbt_ref[t * LANES:(t + 1) * LANES, :] = (r * (B_HEAD_DIM ** -0.5 * LOG2E)).T.astype(BF16)
    for t, y in enumerate(tiles(proj(COL_KB, 256))):
        r = rope128(y)
        kbf_ref[:, t * LANES:(t + 1) * LANES] = r
        kbb_ref[:, t * LANES:(t + 1) * LANES] = r.astype(BF16)
    y = proj(COL_VB, 256)
    vbf_ref[...] = y
    vbb_ref[...] = y.astype(BF16)
    put_values_t(vbt_ref, y)
    for t, y in enumerate(tiles(proj(COL_QI, 256))):
        r = rope64(y)
        qi_ref[:, t * LANES:(t + 1) * LANES] = r.astype(BF16)
        qit_ref[t * LANES:(t + 1) * LANES, :] = r.T.astype(BF16)
    y = proj(COL_KIWI, LANES)
    is_key = lane < IDX_DIM
    kiwi = jnp.where(is_key, rope64(y), y * (IDX_HEADS ** -0.5))
    kiwi_ref[...] = kiwi
    ki_ref[...] = kiwi[:, 0:IDX_DIM].astype(BF16)
    ki2 = jnp.where(is_key, kiwi, pltpu.roll(kiwi, IDX_DIM, 1)).astype(BF16)
    ki4_ref[:, 0:LANES] = ki2
    ki4_ref[:, LANES:2 * LANES] = ki2
    wt_ref[...] = kiwi.T[IDX_DIM:IDX_DIM + 8, :]


def _project(h, w1, tabs):
    n = h.shape[0]

    def row(width):
        return pl.BlockSpec((TILE_ROWS, width), lambda i: (i, 0))

    def col(height):
        return (pl.BlockSpec((height, TILE_ROWS), lambda i: (0, i)), (height, n))

    def vals(heads):
        return (pl.BlockSpec((heads, None, V_ROWS, TILE_ROWS), lambda i: (0, i, 0, 0)),
                (heads, n // TILE_ROWS, V_ROWS, TILE_ROWS))

    def rows(width):
        return (row(width), (n, width))

    outs = [rows(1024) + (BF16,), col(1024) + (BF16,), rows(1024) + (F32,), rows(1024) + (BF16,),
            rows(1024) + (F32,), rows(1024) + (BF16,), vals(A_HEADS) + (BF16,),
            rows(1024) + (BF16,), col(1024) + (BF16,), rows(256) + (F32,), rows(256) + (BF16,),
            rows(256) + (F32,), rows(256) + (BF16,), vals(B_KV_HEADS) + (BF16,),
            rows(256) + (BF16,), col(256) + (BF16,), rows(LANES) + (F32,), rows(IDX_DIM) + (BF16,),
            rows(256) + (BF16,), col(8) + (F32,)]
    return pl.pallas_call(
        _proj_kernel, grid=(n // TILE_ROWS,),
        in_specs=[row(D_MODEL), pl.BlockSpec((D_MODEL, N_PROJ), lambda i: (0, 0))] + [row(LANES)] * 4,
        out_specs=[spec for spec, _, _ in outs],
        out_shape=[jax.ShapeDtypeStruct(shape, dt) for _, shape, dt in outs],
        compiler_params=_params("parallel"), name="in_proj")(h, w1, *tabs)


def _subln(o1, o2, lam, g, lam_init):
    o = o1 - lam * o2
    y = o * lax.rsqrt(jnp.mean(o * o, -1, keepdims=True) + LN_EPS) * g
    return y * (1.0 - lam_init)


def _diff_p_kernel(lam_ref, qt_ref, k_ref, vt_ref, g_ref, o_ref, qq_sc, ta_sc, tb_sc, m_sc, acc_sc, *,
                   lam_init):
    tq = qt_ref.shape[1]
    i = pl.program_id(1)
    qt = qt_ref[...]
    row = lax.broadcasted_iota(I32, (LANES, 1), 0)
    zero = jnp.zeros_like(qt)
    qq_sc[...] = jnp.concatenate([jnp.where(row < A_HEAD_DIM, qt, zero),
                                  jnp.where(row >= A_HEAD_DIM, qt, zero)], axis=1)
    q_pos = i * tq + lax.broadcasted_iota(I32, (1, tq), 1)
    q_chunk = jnp.concatenate([q_pos, q_pos], axis=1) >> 6
    m_sc[...] = jnp.full(m_sc.shape, NEG_INF, F32)
    acc_sc[...] = jnp.zeros(acc_sc.shape, F32)
    n_kv = ((i + 1) * tq + TILE_K - 1) // TILE_K

    def scores(j):
        return _dot(k_ref[_tile_rows(j), :], qq_sc[...])

    def update(t_ref, j):
        _flash_update_t(t_ref, m_sc, acc_sc, [vt_ref[VT_PER_K * j + c] for c in range(VT_PER_K)], tq)

    def masked_scores(j):
        return jnp.where(_chunk_mask_t(j, q_chunk), scores(j), NEG_INF)

    last = n_kv - 1
    ta_sc[...] = masked_scores(0)

    @pl.when(n_kv == 1)
    def _():
        update(ta_sc, 0)

    n_pairs = jnp.maximum(n_kv - 2, 0) // 2

    def pair(r, carry):
        j = 2 * r
        tb_sc[...] = scores(j + 1)
        update(ta_sc, j)
        ta_sc[...] = scores(j + 2)
        update(tb_sc, j + 1)
        return carry

    lax.fori_loop(0, n_pairs, pair, 0)
    j_cur = 2 * n_pairs

    @pl.when((n_kv > 1) & (last - j_cur == 1))
    def _():
        tb_sc[...] = masked_scores(last)
        update(ta_sc, j_cur)
        update(tb_sc, last)

    @pl.when((n_kv > 1) & (last - j_cur == 2))
    def _():
        tb_sc[...] = scores(j_cur + 1)
        update(ta_sc, j_cur)
        ta_sc[...] = masked_scores(last)
        update(tb_sc, j_cur + 1)
        update(ta_sc, last)

    acc = acc_sc[...]
    outs = [acc[0:LANES, mp * tq:(mp + 1) * tq] / acc[LANES:LANES + 1, mp * tq:(mp + 1) * tq] for mp in range(2)]
    o = outs[0] - lam_ref[0] * outs[1]
    y = o * lax.rsqrt(jnp.mean(o * o, axis=0, keepdims=True) + LN_EPS) * g_ref[...] * (1.0 - lam_init)
    o_ref[...] = y.T.astype(BF16)


def _diff_attn_prompt(lam, qat, ka, vat, g_col, lp, lam_init):
    n = ka.shape[0]
    return pl.pallas_call(
        functools.partial(_diff_p_kernel, lam_init=lam_init),
        grid=(A_HEADS, lp // TILE_Q_A),
        in_specs=[pl.BlockSpec(memory_space=pltpu.SMEM),
                  pl.BlockSpec((LANES, TILE_Q_A), lambda h, i: (h, i)),
                  pl.BlockSpec((lp, LANES), lambda h, i: (0, h)),
                  pl.BlockSpec((None, lp // TILE_ROWS, V_ROWS, TILE_ROWS), lambda h, i: (h, 0, 0, 0)),
                  pl.BlockSpec((LANES, 1), lambda h, i: (0, 0))],
        out_specs=pl.BlockSpec((TILE_Q_A, LANES), lambda h, i: (i, h)),
        out_shape=jax.ShapeDtypeStruct((n, A_HEADS * LANES), BF16),
        scratch_shapes=[pltpu.VMEM((LANES, 2 * TILE_Q_A), BF16),
                        pltpu.VMEM((TILE_K, 2 * TILE_Q_A), F32), pltpu.VMEM((TILE_K, 2 * TILE_Q_A), F32),
                        pltpu.VMEM((1, 2 * TILE_Q_A), F32), pltpu.VMEM((V_ROWS, 2 * TILE_Q_A), F32)],
        compiler_params=_params("parallel", "parallel"), name="diff_attn_prompt")(lam, qat, ka, vat, g_col)


def _diff_s_kernel(lam_ref, q_ref, ck_ref, cv_ref, kn_ref, vn_ref, g_ref, prev_ref, o_ref, *, lam_init, n_keys):
    del prev_ref
    ds_ = q_ref.shape[0]
    past = ck_ref.shape[0]
    pad = n_keys - past - ds_
    lane = lax.broadcasted_iota(I32, (1, LANES), 1)
    valid = lax.broadcasted_iota(I32, (1, n_keys), 1) < past + ds_
    for h in range(A_HEADS):
        cols = slice(h * LANES, (h + 1) * LANES)
        q = q_ref[:, cols]
        zero = jnp.zeros_like(q)
        k = jnp.concatenate([ck_ref[:, cols].astype(BF16), kn_ref[:, cols], jnp.zeros((pad, LANES), BF16)], 0)
        v = jnp.concatenate([cv_ref[:, cols].astype(BF16), vn_ref[:, cols], jnp.zeros((pad, LANES), BF16)], 0)
        outs = []
        for qz in (jnp.where(lane < A_HEAD_DIM, q, zero), jnp.where(lane >= A_HEAD_DIM, q, zero)):
            s = jnp.where(valid, _dot_nt(qz, k) * (A_HEAD_DIM ** -0.5), NEG_INF)
            p = jnp.exp(s - jnp.max(s, axis=1, keepdims=True))
            outs.append(_dot(p.astype(BF16), v) / jnp.sum(p, axis=1, keepdims=True))
        o_ref[:, cols] = _subln(outs[0], outs[1], lam_ref[0], g_ref[...], lam_init).astype(BF16)


def _diff_attn_sample(lam, qa, cache_k, cache_v, ka, va, g, ya, layer, lp, lam_init):
    _, nb, past, width = cache_k.shape
    n = qa.shape[0]
    ds_ = (n - lp) // nb
    n_keys = -(-(past + ds_) // LANES) * LANES
    base = lp // ds_
    new = pl.BlockSpec((ds_, width), lambda b: (base + b, 0))
    cache = pl.BlockSpec((None, None, past, width), lambda b: (layer, b, 0, 0))
    return pl.pallas_call(
        functools.partial(_diff_s_kernel, lam_init=lam_init, n_keys=n_keys), grid=(nb,),
        in_specs=[pl.BlockSpec(memory_space=pltpu.SMEM), new, cache, cache, new, new,
                  pl.BlockSpec((1, LANES), lambda b: (0, 0)), pl.BlockSpec(memory_space=pl.ANY)],
        out_specs=new, out_shape=jax.ShapeDtypeStruct(ya.shape, ya.dtype),
        input_output_aliases={7: 0},
        compiler_params=_params("parallel"), name="diff_attn_sample")(lam, qa, cache_k, cache_v, ka, va, g, ya)


def _index_scores(qiz, w_cols, ki4, rows):
    r = _dot_nt(qiz, ki4)
    sc = w_cols[0] * jnp.maximum(r[0:rows], 0.0)
    for h in range(1, IDX_HEADS):
        sc = sc + w_cols[h] * jnp.maximum(r[h * rows:(h + 1) * rows], 0.0)
    return sc


def _stack_idx_queries(qi, rows):
    lane = lax.broadcasted_iota(I32, (1, IDX_HEADS * IDX_DIM), 1)
    zero = jnp.zeros_like(qi)
    return [jnp.where((lane >= IDX_DIM * h) & (lane < IDX_DIM * (h + 1)), qi, zero) for h in range(IDX_HEADS)]


def _selected(key, t, need, seen, u):
    eq = key == t
    prefix = _dot(jnp.where(eq, 1.0, 0.0).astype(BF16), u)
    sel = ((key > t) | (eq & (seen + prefix <= need))) & (key > KEY_HALF_NEG)
    return sel, seen + prefix[:, -1:]


def _dsa_p_kernel(qbt_ref, qit_ref, wt_ref, kb_ref, vbt_ref, ki_ref, tri_ref, o_ref,
                  key_sc, qq_sc, ta_sc, tb_sc, sel_sc, m_sc, acc_sc, *, n_sel):
    tq = qbt_ref.shape[1]
    i = pl.program_id(0)
    n_kv = ((i + 1) * tq + TILE_K - 1) // TILE_K
    w = wt_ref[...] * (IDX_DIM ** -0.5)
    w_rows = [w[h:h + 1] for h in range(IDX_HEADS)]
    q_chunk = (i * tq + lax.broadcasted_iota(I32, (1, tq), 1)) >> 6

    def score_tile(j, top):
        ki = ki_ref[_tile_rows(j), :]
        sc = None
        for h in range(IDX_HEADS):
            r = w_rows[h] * jnp.maximum(_dot(ki, qit_ref[h * IDX_DIM:(h + 1) * IDX_DIM, :]), 0.0)
            sc = r if sc is None else sc + r
        key = _sort_key(jnp.where(_chunk_mask_t(j, q_chunk), sc, NEG_INF))
        key_sc[j] = key
        return jnp.maximum(top, jnp.max(key, axis=0, keepdims=True))

    top_key = lax.fori_loop(0, n_kv, score_tile, jnp.full((1, tq), INT_MIN, I32))

    def count_ge(c):
        def body(j, acc):
            parts = [None] * COUNT_CHAINS
            for n, r in enumerate(range(0, TILE_K, COUNT_ROWS)):
                hit = jnp.where(key_sc[j, r:r + COUNT_ROWS, :] >= c, 1.0, 0.0)
                parts[n % COUNT_CHAINS] = hit if parts[n % COUNT_CHAINS] is None else parts[n % COUNT_CHAINS] + hit
            return acc + functools.reduce(lambda a, b: a + b, parts)
        return jnp.sum(lax.fori_loop(0, n_kv, body, jnp.zeros((COUNT_ROWS, tq), F32)), axis=0, keepdims=True)

    thr = _kth_largest_key_bracketed(count_ge, top_key, float(n_sel))
    need = float(n_sel) - count_ge(thr + 1)
    need = jnp.where(thr > KEY_HALF_NEG, need, 0.0)
    floor_key = jnp.maximum(thr, KEY_HALF_NEG)

    for c in range(B_KV_HEADS):
        for g in range(B_GROUP):
            hh = c * B_GROUP + g
            qq_sc[c, :, g * tq:(g + 1) * tq] = qbt_ref[hh * LANES:(hh + 1) * LANES, :]
    m_sc[...] = jnp.full(m_sc.shape, NEG_INF, F32)
    acc_sc[...] = jnp.zeros(acc_sc.shape, F32)

    def put_scores(t_ref, j, c):
        s = _dot(kb_ref[_tile_rows(j), c * LANES:(c + 1) * LANES], qq_sc[c])
        sel = sel_sc[...] > 0.5
        for g in range(B_GROUP):
            t_ref[:, g * tq:(g + 1) * tq] = jnp.where(sel, s[:, g * tq:(g + 1) * tq], NEG_INF)

    def update(t_ref, j, c):
        _flash_update_t(t_ref, m_sc.at[c], acc_sc.at[c],
                        [vbt_ref[c, VT_PER_K * j + cc] for cc in range(VT_PER_K)], tq)

    def select(j, seen):
        key = key_sc[j]
        tie = key == thr
        half = TILE_K // 2
        ones = jnp.where(tie, 1.0, 0.0).astype(BF16)
        for r in range(2):
            rows = slice(r * half, (r + 1) * half)
            prefix = seen + _dot(tri_ref[...], ones[rows])
            sel_sc[rows, :] = jnp.where((key[rows] > floor_key) | (tie[rows] & (prefix <= need)), 1.0, 0.0)
            seen = prefix[half - 1:half, :]
        return seen

    seen0 = select(0, jnp.zeros((1, tq), F32))
    put_scores(ta_sc, 0, 0)

    def attend(j, seen):
        nxt = jnp.minimum(j + 1, n_kv - 1)
        put_scores(tb_sc, j, 1)
        update(ta_sc, j, 0)
        seen = select(nxt, seen)
        put_scores(ta_sc, nxt, 0)
        update(tb_sc, j, 1)
        return seen

    lax.fori_loop(0, n_kv, attend, seen0)
    for c in range(B_KV_HEADS):
        acc = acc_sc[c]
        for g in range(B_GROUP):
            hh = c * B_GROUP + g
            a = acc[:, g * tq:(g + 1) * tq]
            o_ref[:, hh * LANES:(hh + 1) * LANES] = (a[0:LANES] / a[LANES:LANES + 1]).T.astype(BF16)


def _dsa_prompt(qbt, qit, wt, kb, vbt, ki, tri, lp, n_sel):
    n = kb.shape[0]
    tq = TILE_Q_B

    def col(height):
        return pl.BlockSpec((height, tq), lambda i: (0, i))

    def resident(shape):
        return pl.BlockSpec(shape, lambda i: (0,) * len(shape), pipeline_mode=pl.Buffered(1))

    return pl.pallas_call(
        functools.partial(_dsa_p_kernel, n_sel=n_sel), grid=(lp // tq,),
        in_specs=[col(1024), col(256), col(8), resident((lp, 256)),
                  resident((B_KV_HEADS, lp // TILE_ROWS, V_ROWS, TILE_ROWS)), resident((lp, IDX_DIM)),
                  resident((TILE_K // 2, TILE_K // 2))],
        out_specs=pl.BlockSpec((tq, 1024), lambda i: (i, 0)),
        out_shape=jax.ShapeDtypeStruct((n, 1024), BF16),
        scratch_shapes=[pltpu.VMEM((lp // TILE_K, TILE_K, tq), I32),
                        pltpu.VMEM((B_KV_HEADS, LANES, B_GROUP * tq), BF16),
                        pltpu.VMEM((TILE_K, B_GROUP * tq), F32), pltpu.VMEM((TILE_K, B_GROUP * tq), F32),
                        pltpu.VMEM((TILE_K, tq), F32),
                        pltpu.VMEM((B_KV_HEADS, 1, B_GROUP * tq), F32),
                        pltpu.VMEM((B_KV_HEADS, V_ROWS, B_GROUP * tq), F32)],
        compiler_params=_params("parallel"), name="dsa_prompt")(qbt, qit, wt, kb, vbt, ki, tri)


def _dsa_s_kernel(qb_ref, qi_ref, kiwi_ref, ckb_ref, cvb_ref, cki_ref, kbn_ref, vbn_ref, ki4n_ref,
                  rep_ref, u_ref, prev_ref, o_ref, *, n_sel, n_keys):
    del prev_ref
    tq = qb_ref.shape[0]
    past = ckb_ref.shape[0]
    pad = n_keys - past - tq
    width = B_KV_HEADS * B_HEAD_DIM
    kb = jnp.concatenate([ckb_ref[...].astype(BF16), kbn_ref[...], jnp.zeros((pad, width), BF16)], 0)
    vb = jnp.concatenate([cvb_ref[...].astype(BF16), vbn_ref[...], jnp.zeros((pad, width), BF16)], 0)
    ki4_past = _dot(cki_ref[...].astype(BF16), rep_ref[...]).astype(BF16)
    ki4 = jnp.concatenate([ki4_past, ki4n_ref[...], jnp.zeros((pad, width), BF16)], 0)
    qiz = jnp.concatenate(_stack_idx_queries(qi_ref[...], tq), 0)
    wq = kiwi_ref[...] * (IDX_DIM ** -0.5)
    w_cols = [wq[:, IDX_DIM + h:IDX_DIM + h + 1] for h in range(IDX_HEADS)]
    sc = _index_scores(qiz, w_cols, ki4, tq)
    valid = lax.broadcasted_iota(I32, (1, n_keys), 1) < past + tq
    key = _sort_key(jnp.where(valid, sc, NEG_INF))

    def count_ge(c):
        return jnp.sum(jnp.where(key >= c, 1.0, 0.0), axis=1, keepdims=True)

    t = _kth_largest_key(count_ge, (tq, 1), float(n_sel))
    need = float(n_sel) - count_ge(t + 1)
    sel, _ = _selected(key, t, need, jnp.zeros((tq, 1), F32), u_ref[...])
    for c in range(B_KV_HEADS):
        cols = slice(c * LANES, (c + 1) * LANES)
        q = jnp.concatenate([qb_ref[:, (c * B_GROUP + g) * LANES:(c * B_GROUP + g + 1) * LANES]
                             for g in range(B_GROUP)], 0)
        s_all = _dot_nt(q, kb[:, cols]) * (B_HEAD_DIM ** -0.5)
        for g in range(B_GROUP):
            hh = c * B_GROUP + g
            s = jnp.where(sel, s_all[g * tq:(g + 1) * tq], NEG_INF)
            p = jnp.exp(s - jnp.max(s, axis=1, keepdims=True))
            o = _dot(p.astype(BF16), vb[:, cols]) / jnp.sum(p, axis=1, keepdims=True)
            o_ref[:, hh * LANES:(hh + 1) * LANES] = o.astype(BF16)


def _dsa_sample(qb, qi, kiwi, cache_k, cache_v, cache_idx, kb, vb, ki4, rep, u, yb, layer, lp, n_sel):
    _, nb, past, _ = cache_k.shape
    n = qb.shape[0]
    ds_ = (n - lp) // nb
    n_keys = u.shape[0]
    base = lp // ds_

    def new(width):
        return pl.BlockSpec((ds_, width), lambda b: (base + b, 0))

    def cache(width):
        return pl.BlockSpec((None, None, past, width), lambda b: (layer, b, 0, 0))

    def const(shape):
        return pl.BlockSpec(shape, lambda b: (0, 0))

    return pl.pallas_call(
        functools.partial(_dsa_s_kernel, n_sel=n_sel, n_keys=n_keys), grid=(nb,),
        in_specs=[new(1024), new(256), new(LANES), cache(256), cache(256), cache(IDX_DIM),
                  new(256), new(256), new(256), const(rep.shape), const(u.shape),
                  pl.BlockSpec(memory_space=pl.ANY)],
        out_specs=new(1024), out_shape=jax.ShapeDtypeStruct(yb.shape, yb.dtype),
        input_output_aliases={11: 0},
        compiler_params=_params("parallel"), name="dsa_sample")(
            qb, qi, kiwi, cache_k, cache_v, cache_idx, kb, vb, ki4, rep, u, yb)


def _merge_kernel(h_ref, ya_ref, yb_ref, wga_ref, wgb_ref, wpa_ref, wpb_ref, wo_ref, g_ref, b_ref,
                  wr_ref, bias_ref, o_ref, gates_ref):
    h = h_ref[...]
    hb = h.astype(BF16)
    mixed = (jax.nn.sigmoid(_dot(hb, wga_ref[...])) * _dot(ya_ref[...], wpa_ref[...])
             + jax.nn.sigmoid(_dot(hb, wgb_ref[...])) * _dot(yb_ref[...], wpb_ref[...]))
    y = _dot(mixed.astype(BF16), wo_ref[...])
    h1 = _layernorm(DEEPNORM_ALPHA * h + y, g_ref[...], b_ref[...])
    o_ref[...] = h1
    gates_ref[...] = _route(jax.nn.sigmoid(_dot(h1.astype(BF16), wr_ref[...])), bias_ref[...])


def _merge(h, ya, yb, wga, wgb, wpa, wpb, wo, g, b, wr, bias):
    n, d = h.shape
    row = pl.BlockSpec((TILE_ROWS, d), lambda i: (i, 0))
    mat = pl.BlockSpec((d, d), lambda i: (0, 0))
    vec = pl.BlockSpec((1, d), lambda i: (0, 0))
    return pl.pallas_call(
        _merge_kernel, grid=(n // TILE_ROWS,),
        in_specs=[row, row, row] + [mat] * 5 + [vec, vec, pl.BlockSpec((d, N_EXPERTS), lambda i: (0, 0)),
                                                 pl.BlockSpec((1, N_EXPERTS), lambda i: (0, 0))],
        out_specs=[row, pl.BlockSpec((TILE_ROWS, N_EXPERTS), lambda i: (i, 0))],
        out_shape=[jax.ShapeDtypeStruct((n, d), F32), jax.ShapeDtypeStruct((n, N_EXPERTS), F32)],
        compiler_params=_params("parallel"), name="merge_ln1_route")(
            h, ya, yb, wga, wgb, wpa, wpb, wo, g.reshape(1, d), b.reshape(1, d), wr,
            bias.reshape(1, N_EXPERTS))


def _route(s, bias):
    sel = s + bias
    lane = lax.broadcasted_iota(I32, (1, N_EXPERTS), 1)
    lane_f = lane.astype(F32)
    group = lane >> 3
    neg = -jnp.inf

    def first_argmax(x, m):
        return jnp.min(jnp.where(x == m, lane_f, float(N_EXPERTS)), axis=1, keepdims=True)

    g_score = []
    for g in range(N_GROUPS):
        x = jnp.where(group == g, sel, neg)
        m1 = jnp.max(x, axis=1, keepdims=True)
        x = jnp.where(lane_f == first_argmax(x, m1), neg, x)
        g_score.append(m1 + jnp.max(x, axis=1, keepdims=True))
    e_keep = None
    for g in range(N_GROUPS):
        ahead = jnp.zeros_like(g_score[g])
        for o in range(N_GROUPS):
            if o != g:
                beats = (g_score[o] > g_score[g]) | ((g_score[o] == g_score[g]) & (o < g))
                ahead = ahead + jnp.where(beats, 1.0, 0.0)
        keep = (group == g) & (ahead < float(TOPK_GROUPS))
        e_keep = keep if e_keep is None else (e_keep | keep)
    x = jnp.where(e_keep, sel, NEG_INF)
    chosen = None
    for _ in range(TOP_K):
        hit = lane_f == first_argmax(x, jnp.max(x, axis=1, keepdims=True))
        chosen = hit if chosen is None else (chosen | hit)
        x = jnp.where(hit, neg, x)
    w = jnp.where(chosen, s, 0.0)
    return w / jnp.sum(w, axis=1, keepdims=True) * ROUTE_SCALE


def _silu(x):
    return x * jax.nn.sigmoid(x)


def _moe_kernel(h_ref, gates_ref, wg_ref, wu_ref, wd_ref, sg_ref, su_ref, sd_ref, g_ref, b_ref, o_ref,
                xb_sc, acc_sc):
    step = pl.program_id(1)

    @pl.when(step == 0)
    def _():
        xb = h_ref[...].astype(BF16)
        xb_sc[...] = xb
        a = _silu(_dot(xb, sg_ref[...])) * _dot(xb, su_ref[...])
        acc_sc[...] = _dot(a.astype(BF16), sd_ref[...])

    xb = xb_sc[...]
    gates = gates_ref[...]
    lane = lax.broadcasted_iota(I32, (1, N_EXPERTS), 1)
    acts = []
    for e in range(EXPERT_BLOCK):
        gate = jnp.sum(jnp.where(lane == step * EXPERT_BLOCK + e, gates, 0.0), axis=1, keepdims=True)
        a = _silu(_dot(xb, wg_ref[e])) * _dot(xb, wu_ref[e]) * gate
        acts.append(a.astype(BF16))
    wd = wd_ref[...]
    acc_sc[...] += _dot(jnp.concatenate(acts, axis=1), wd.reshape(wd.shape[0] * wd.shape[1], wd.shape[2]))

    @pl.when(step == pl.num_programs(1) - 1)
    def _():
        o_ref[...] = _layernorm(DEEPNORM_ALPHA * h_ref[...] + acc_sc[...], g_ref[...], b_ref[...])


def _moe(h, gates, wg, wu, wd, sg, su, sd, g, b):
    n, d = h.shape
    f = wg.shape[-1]
    row = pl.BlockSpec((TILE_MOE, d), lambda r, e: (r, 0))
    vec = pl.BlockSpec((1, d), lambda r, e: (0, 0))
    return pl.pallas_call(
        _moe_kernel, grid=(n // TILE_MOE, N_EXPERTS // EXPERT_BLOCK),
        in_specs=[row, pl.BlockSpec((TILE_MOE, N_EXPERTS), lambda r, e: (r, 0)),
                  pl.BlockSpec((EXPERT_BLOCK, d, f), lambda r, e: (e, 0, 0)),
                  pl.BlockSpec((EXPERT_BLOCK, d, f), lambda r, e: (e, 0, 0)),
                  pl.BlockSpec((EXPERT_BLOCK, f, d), lambda r, e: (e, 0, 0)),
                  pl.BlockSpec((d, sg.shape[1]), lambda r, e: (0, 0)),
                  pl.BlockSpec((d, sg.shape[1]), lambda r, e: (0, 0)),
                  pl.BlockSpec((sg.shape[1], d), lambda r, e: (0, 0)), vec, vec],
        out_specs=row, out_shape=jax.ShapeDtypeStruct((n, d), F32),
        scratch_shapes=[pltpu.VMEM((TILE_MOE, d), BF16), pltpu.VMEM((TILE_MOE, d), F32)],
        compiler_params=_params("parallel", "arbitrary"), name="moe_ln2")(
            h, gates, wg, wu, wd, sg, su, sd, g.reshape(1, d), b.reshape(1, d))


def _rope_tables(pos):
    def table(half, reps):
        inv = ROPE_THETA ** (-jnp.arange(half, dtype=F32) / half)
        ang = pos.astype(F32)[:, None] * inv[None, :]
        cos, sin = jnp.cos(ang), jnp.sin(ang)
        return (jnp.tile(jnp.concatenate([cos, cos], -1), (1, reps)),
                jnp.tile(jnp.concatenate([-sin, sin], -1), (1, reps)))
    c64, s64 = table(A_HEAD_DIM // 2, 2)
    c128, s128 = table(B_HEAD_DIM // 2, 1)
    return c64, s64, c128, s128


def kernel(x_prompt, x_sample, cache_a_k, cache_a_v, cache_b_k, cache_b_v, cache_b_idx, meta_tokens, ln_in_g, ln_in_b, w_in, lam_q1, lam_k1, lam_q2, lam_k2, subln_g, w_proj_a, w_proj_b, w_out, ln1_g, ln1_b, w_router, router_bias, w_expert_gate, w_expert_up, w_expert_down, w_shared_gate, w_shared_up, w_shared_down, ln2_g, ln2_b):
    batch, seq, d = x_prompt.shape
    nb, ds_, _ = x_sample.shape
    depth, _, past = cache_a_k.shape[:3]
    assert batch == 1 and d == D_MODEL and seq % TILE_K == 0 and ds_ == CHUNK
    lp = FRONT_PAD + N_META + seq
    ns = nb * ds_
    n = lp + ns
    assert n % TILE_MOE == 0 and past % 16 == 0
    n_sel_p = min(TOPK_MAX, seq // 4)
    n_sel_s = min(TOPK_MAX, (past + ds_) // 4)
    n_keys_s = -(-(past + ds_) // LANES) * LANES

    x_all = jnp.concatenate([jnp.zeros((FRONT_PAD, d), F32), meta_tokens.astype(F32), x_prompt[0],
                             x_sample.reshape(ns, d)], 0)
    pos = jnp.concatenate([jnp.maximum(jnp.arange(lp, dtype=jnp.int32) - FRONT_PAD, 0),
                           jnp.tile(past + jnp.arange(ds_, dtype=jnp.int32), nb)])
    tabs = _rope_tables(pos)
    tri_p = jnp.tril(jnp.ones((TILE_K // 2, TILE_K // 2), BF16))
    u_s = jnp.triu(jnp.ones((n_keys_s, n_keys_s), BF16))
    rep = jnp.tile(jnp.eye(IDX_DIM, dtype=BF16), (1, IDX_HEADS))
    ck_a = cache_a_k.reshape(depth, nb, past, -1)
    cv_a = cache_a_v.reshape(depth, nb, past, -1)
    ck_b = cache_b_k.reshape(depth, nb, past, -1)
    cv_b = cache_b_v.reshape(depth, nb, past, -1)

    h = _ln_rows(x_all, ln_in_g, ln_in_b)
    rows = []
    for l in range(depth):
        lam_init = 0.8 - 0.6 * math.exp(-0.3 * l)
        lam = (jnp.exp(jnp.sum(lam_q1[l].astype(F32) * lam_k1[l].astype(F32)))
               - jnp.exp(jnp.sum(lam_q2[l].astype(F32) * lam_k2[l].astype(F32))) + lam_init).reshape(1)
        w1 = jnp.concatenate([w_in[l][:, :N_RAW_PROJ], jnp.zeros((d, N_PROJ - N_RAW_PROJ), F32)], 1).astype(BF16)
        wga = w_in[l][:, N_RAW_PROJ:N_RAW_PROJ + d].astype(BF16)
        wgb = w_in[l][:, N_RAW_PROJ + d:].astype(BF16)
        (qa, qat, ka_f, ka, va_f, va, vat, qb, qbt, kb_f, kb, vb_f, vb, vbt, qi, qit, kiwi, ki, ki4,
         wt) = _project(h, w1, tabs)
        rows.append((ka_f, va_f, kb_f, vb_f, kiwi))
        g = subln_g[l].astype(F32)
        ya = _diff_attn_prompt(lam, qat, ka, vat, g.reshape(LANES, 1), lp, lam_init)
        ya = _diff_attn_sample(lam, qa, ck_a, cv_a, ka, va, g.reshape(1, LANES), ya, l, lp, lam_init)
        yb = _dsa_prompt(qbt, qit, wt, kb, vbt, ki, tri_p, lp, n_sel_p)
        yb = _dsa_sample(qb, qi, kiwi, ck_b, cv_b, cache_b_idx, kb, vb, ki4, rep, u_s, yb, l, lp, n_sel_s)
        h, gates = _merge(h, ya, yb, wga, wgb, w_proj_a[l].astype(BF16), w_proj_b[l].astype(BF16),
                          w_out[l].astype(BF16), ln1_g[l], ln1_b[l], w_router[l].astype(BF16),
                          router_bias[l].astype(F32))
        h = _moe(h, gates, w_expert_gate[l].astype(BF16), w_expert_up[l].astype(BF16),
                 w_expert_down[l].astype(BF16), w_shared_gate[l].astype(BF16), w_shared_up[l].astype(BF16),
                 w_shared_down[l].astype(BF16), ln2_g[l], ln2_b[l])

    p0 = FRONT_PAD
    y_prompt = h[p0 + N_META:lp].reshape(1, seq, d)
    y_sample = h[lp:].reshape(nb, ds_, d)

    def stack(idx, width, tail):
        return (jnp.stack([r[idx][p0:lp, :width] for r in rows]).reshape((depth, 1, lp - p0) + tail),
                jnp.stack([r[idx][lp:, :width] for r in rows]).reshape((depth, nb, ds_) + tail))

    ak_p, ak_s = stack(0, 1024, (2 * A_HEADS, A_HEAD_DIM))
    av_p, av_s = stack(1, 1024, (A_HEADS, 2 * A_HEAD_DIM))
    bk_p, bk_s = stack(2, 256, (B_KV_HEADS, B_HEAD_DIM))
    bv_p, bv_s = stack(3, 256, (B_KV_HEADS, B_HEAD_DIM))
    bi_p, bi_s = stack(4, IDX_DIM, (IDX_DIM,))
    return (y_prompt, y_sample, ak_p, av_p, bk_p, bv_p, bi_p, ak_s, av_s, bk_s, bv_s, bi_s)
```

```python
import functools
import math

import numpy as np
import jax
import jax.numpy as jnp
from jax import lax
from jax.experimental import pallas as pl
from jax.experimental.pallas import tpu as pltpu

F32 = jnp.float32
BF16 = jnp.bfloat16
I32 = jnp.int32

MODEL_DEPTH = 4
CHUNK = 64
N_META = 16
ROPE_THETA = 10000.0
LN_EPS = 1e-5
NEG_INF = -1e30
DEEPNORM_ALPHA = (2 * MODEL_DEPTH) ** 0.25
A_HEADS = 8
A_HEAD_DIM = 64
B_HEADS = 8
B_KV_HEADS = 2
B_GROUP = B_HEADS // B_KV_HEADS
B_HEAD_DIM = 128
IDX_HEADS = 4
IDX_DIM = 64
TOPK_MAX = 256
N_EXPERTS = 64
TOP_K = 8
N_GROUPS = 8
GROUP_SIZE = N_EXPERTS // N_GROUPS
TOPK_GROUPS = 4
ROUTE_SCALE = 2.5

LANES = 128
D_MODEL = 1024
N_PROJ = 4992
COL_QA, COL_KA, COL_VA, COL_QB, COL_KB, COL_VB, COL_QI, COL_KIWI = (
    0, 1024, 2048, 3072, 4096, 4352, 4608, 4864)
N_RAW_PROJ = 4932

TILE_K = 512
FRONT_PAD = TILE_K - N_META
TILE_Q_A = 512
TILE_Q_B = 256
TILE_ROWS = 256
TILE_MOE = 512
ROUTE_ROWS = 64
EXPERT_BLOCK = 4
VMEM_LIMIT = 56 * 1024 * 1024
BIG_CHUNK = 1 << 30
INT_MIN = -(2 ** 31)
LOG2E = 1.4426950408889634
V_ROWS = 144
VT_PER_K = TILE_K // TILE_ROWS
COUNT_ROWS = 16
COUNT_CHAINS = 4
SOFTMAX_ROWS = 128


def _f32_key(v):
    b = int(np.float32(v).view(np.int32))
    return b ^ ((b >> 31) & 0x7FFFFFFF)


KEY_HALF_NEG = _f32_key(0.5 * NEG_INF)


def _dot(a, b):
    return jnp.dot(a, b, preferred_element_type=F32)


def _dot_nt(a, b):
    return lax.dot_general(a, b, (((1,), (1,)), ((), ())), preferred_element_type=F32)


def _layernorm(x, g, b):
    mu = jnp.mean(x, -1, keepdims=True)
    xc = x - mu
    var = jnp.mean(xc * xc, -1, keepdims=True)
    return xc * lax.rsqrt(var + LN_EPS) * g + b


def _sort_key(x):
    bits = lax.bitcast_convert_type(x, I32)
    return bits ^ ((bits >> 31) & 0x7FFFFFFF)


def _kth_largest_key(count_ge, shape, k):
    def body(b, tu):
        cand = tu | jnp.left_shift(jnp.int32(1), 31 - b)
        c = count_ge(cand ^ INT_MIN)
        return jnp.where(c >= k, cand, tu)
    tu = lax.fori_loop(0, 32, body, jnp.zeros(shape, I32))
    return tu ^ INT_MIN


def _kth_largest_key_bracketed(count_ge, top_key, k):
    def flags(b):
        return jnp.where(b, 1.0, 0.0)

    def any_set(f):
        return jnp.max(f) > 0.0

    enough = count_ge(jnp.full_like(top_key, KEY_HALF_NEG + 1)) >= k
    top_tie = enough & (count_ge(top_key) >= k)
    lo = jnp.where(top_tie, top_key, INT_MIN)
    hi = top_key
    done = top_tie | ~enough

    def gallop_cond(s):
        return (s[0] < 5) & any_set(s[3])

    def gallop(s):
        g, lo, hi, want = s
        need_lo = want > 0.5
        step = jnp.left_shift(jnp.int32(1 << 23), 2 * g)
        cand = jnp.where((g < 4) & (top_key >= INT_MIN + step), top_key - step, INT_MIN)
        ok = count_ge(cand) >= k
        lo = jnp.where(need_lo & ok, cand, lo)
        hi = jnp.where(need_lo & ~ok, cand, hi)
        return g + 1, lo, hi, flags(need_lo & ~ok)

    _, lo, hi, _ = lax.while_loop(gallop_cond, gallop, (jnp.int32(0), lo, hi, flags(~done)))

    def bisect_cond(s):
        return any_set(s[2])

    def bisect(s):
        lo, hi, open_f = s
        open_ = open_f > 0.5
        mid = lo + lax.shift_right_logical(hi - lo, jnp.full_like(lo, 1))
        c = count_ge(mid)
        ge = c >= k
        lo = jnp.where(open_ & ge, mid, lo)
        hi = jnp.where(open_ & ~ge, mid, hi)
        return lo, hi, flags(open_ & (c != k) & (hi - lo != 1))

    lo, _, _ = lax.while_loop(bisect_cond, bisect, (lo, hi, flags(~done & (hi - lo != 1))))
    return lo


def _tile_rows(j):
    return pl.ds(pl.multiple_of(j * TILE_K, TILE_K), TILE_K)


def _chunk_mask_t(j, q_chunk):
    k_idx = j * TILE_K + lax.broadcasted_iota(I32, (TILE_K, 1), 0)
    return jnp.where(k_idx >= FRONT_PAD, k_idx >> 6, BIG_CHUNK) <= q_chunk


def _flash_update_t(t_ref, m_ref, acc_ref, vt_tiles, width):
    chunks = [slice(r, r + SOFTMAX_ROWS) for r in range(0, TILE_K, SOFTMAX_ROWS)]
    for b in range(t_ref.shape[1] // width):
        cols = slice(b * width, (b + 1) * width)
        m_prev = m_ref[:, cols]
        m_new = m_prev
        for rows in chunks:
            m_new = jnp.maximum(m_new, jnp.max(t_ref[rows, cols], axis=0, keepdims=True))
        p = jnp.concatenate([jnp.exp2(t_ref[rows, cols] - m_new).astype(BF16) for rows in chunks], axis=0)
        pv = _dot(vt_tiles[0], p[0:TILE_ROWS])
        for c in range(1, VT_PER_K):
            pv = pv + _dot(vt_tiles[c], p[c * TILE_ROWS:(c + 1) * TILE_ROWS])
        acc_ref[:, cols] = jnp.exp2(m_prev - m_new) * acc_ref[:, cols] + pv
        m_ref[:, cols] = m_new


def _params(*sem):
    return pltpu.CompilerParams(dimension_semantics=sem, vmem_limit_bytes=VMEM_LIMIT)


def _ln_kernel(x_ref, g_ref, b_ref, o_ref):
    o_ref[...] = _layernorm(x_ref[...], g_ref[...], b_ref[...])


def _ln_rows(x, g, b):
    n, d = x.shape
    row = pl.BlockSpec((TILE_ROWS, d), lambda i: (i, 0))
    vec = pl.BlockSpec((1, d), lambda i: (0, 0))
    return pl.pallas_call(
        _ln_kernel, grid=(n // TILE_ROWS,), in_specs=[row, vec, vec], out_specs=row,
        out_shape=jax.ShapeDtypeStruct((n, d), F32), compiler_params=_params("parallel"),
        name="ln_in")(x, g.reshape(1, d), b.reshape(1, d))


def _proj_kernel(h_ref, w_ref, c64_ref, s64_ref, c128_ref, s128_ref,
                 qa_ref, qat_ref, kaf_ref, kab_ref, vaf_ref, vab_ref, vat_ref, qb_ref, qbt_ref,
                 kbf_ref, kbb_ref, vbf_ref, vbb_ref, vbt_ref, qi_ref, qit_ref, kiwi_ref, ki_ref,
                 ki4_ref, wt_ref):
    hb = h_ref[...].astype(BF16)
    c64, s64, c128, s128 = c64_ref[...], s64_ref[...], c128_ref[...], s128_ref[...]
    lane = lax.broadcasted_iota(I32, c64.shape, 1)
    low_half = (lane & (A_HEAD_DIM // 2)) == 0

    def rope64(y):
        swapped = jnp.where(low_half, pltpu.roll(y, LANES - 32, 1), pltpu.roll(y, 32, 1))
        return y * c64 + swapped * s64

    def rope128(y):
        return y * c128 + pltpu.roll(y, 64, 1) * s128

    def proj(col, width):
        return _dot(hb, w_ref[:, col:col + width])

    def tiles(y):
        return [y[:, t * LANES:(t + 1) * LANES] for t in range(y.shape[1] // LANES)]

    def put_values_t(vt_ref, y):
        for t, v in enumerate(tiles(y)):
            vt_ref[t, 0:LANES, :] = v.T.astype(BF16)
            vt_ref[t, LANES:V_ROWS, :] = jnp.ones((V_ROWS - LANES, v.shape[0]), BF16)

    for t, y in enumerate(tiles(proj(COL_QA, 1024))):
        r = rope64(y)
        qa_ref[:, t * LANES:(t + 1) * LANES] = r.astype(BF16)
        qat_ref[t * LANES:(t + 1) * LANES, :] = (r * (A_HEAD_DIM ** -0.5 * LOG2E)).T.astype(BF16)
    for t, y in enumerate(tiles(proj(COL_KA, 1024))):
        r = rope64(y)
        kaf_ref[:, t * LANES:(t + 1) * LANES] = r
        kab_ref[:, t * LANES:(t + 1) * LANES] = r.astype(BF16)
    y = proj(COL_VA, 1024)
    vaf_ref[...] = y
    vab_ref[...] = y.astype(BF16)
    put_values_t(vat_ref, y)
    for t, y in enumerate(tiles(proj(COL_QB, 1024))):
        r = rope128(y)
        qb_ref[:, t * LANES:(t + 1) * LANES] = r.astype(BF16)
        qbt_ref[t * LANES:(t + 1) * LANES, :] = (r * (B_HEAD_DIM ** -0.5 * LOG2E)).T.astype(BF16)
    for t, y in enumerate(tiles(proj(COL_KB, 256))):
        r = rope128(y)
        kbf_ref[:, t * LANES:(t + 1) * LANES] = r
        kbb_ref[:, t * LANES:(t + 1) * LANES] = r.astype(BF16)
    y = proj(COL_VB, 256)
    vbf_ref[...] = y
    vbb_ref[...] = y.astype(BF16)
    put_values_t(vbt_ref, y)
    for t, y in enumerate(tiles(proj(COL_QI, 256))):
        r = rope64(y)
        qi_ref[:, t * LANES:(t + 1) * LANES] = r.astype(BF16)
        qit_ref[t * LANES:(t + 1) * LANES, :] = r.T.astype(BF16)
    y = proj(COL_KIWI, LANES)
    is_key = lane < IDX_DIM
    kiwi = jnp.where(is_key, rope64(y), y * (IDX_HEADS ** -0.5))
    kiwi_ref[...] = kiwi
    ki_ref[...] = kiwi[:, 0:IDX_DIM].astype(BF16)
    ki2 = jnp.where(is_key, kiwi, pltpu.roll(kiwi, IDX_DIM, 1)).astype(BF16)
    ki4_ref[:, 0:LANES] = ki2
    ki4_ref[:, LANES:2 * LANES] = ki2
    wt_ref[...] = kiwi.T[IDX_DIM:IDX_DIM + 8, :]


def _project(h, w1, tabs):
    n = h.shape[0]

    def row(width):
        return pl.BlockSpec((TILE_ROWS, width), lambda i: (i, 0))

    def col(height):
        return (pl.BlockSpec((height, TILE_ROWS), lambda i: (0, i)), (height, n))

    def vals(heads):
        return (pl.BlockSpec((heads, None, V_ROWS, TILE_ROWS), lambda i: (0, i, 0, 0)),
                (heads, n // TILE_ROWS, V_ROWS, TILE_ROWS))

    def rows(width):
        return (row(width), (n, width))

    outs = [rows(1024) + (BF16,), col(1024) + (BF16,), rows(1024) + (F32,), rows(1024) + (BF16,),
            rows(1024) + (F32,), rows(1024) + (BF16,), vals(A_HEADS) + (BF16,),
            rows(1024) + (BF16,), col(1024) + (BF16,), rows(256) + (F32,), rows(256) + (BF16,),
            rows(256) + (F32,), rows(256) + (BF16,), vals(B_KV_HEADS) + (BF16,),
            rows(256) + (BF16,), col(256) + (BF16,), rows(LANES) + (F32,), rows(IDX_DIM) + (BF16,),
            rows(256) + (BF16,), col(8) + (F32,)]
    return pl.pallas_call(
        _proj_kernel, grid=(n // TILE_ROWS,),
        in_specs=[row(D_MODEL), pl.BlockSpec((D_MODEL, N_PROJ), lambda i: (0, 0))] + [row(LANES)] * 4,
        out_specs=[spec for spec, _, _ in outs],
        out_shape=[jax.ShapeDtypeStruct(shape, dt) for _, shape, dt in outs],
        compiler_params=_params("parallel"), name="in_proj")(h, w1, *tabs)


def _subln(o1, o2, lam, g, lam_init):
    o = o1 - lam * o2
    y = o * lax.rsqrt(jnp.mean(o * o, -1, keepdims=True) + LN_EPS) * g
    return y * (1.0 - lam_init)


def _diff_p_kernel(lam_ref, qt_ref, k_ref, vt_ref, g_ref, o_ref, qq_sc, ta_sc, tb_sc, m_sc, acc_sc, *,
                   lam_init):
    tq = qt_ref.shape[1]
    i = pl.program_id(1)
    qt = qt_ref[...]
    row = lax.broadcasted_iota(I32, (LANES, 1), 0)
    zero = jnp.zeros_like(qt)
    qq_sc[...] = jnp.concatenate([jnp.where(row < A_HEAD_DIM, qt, zero),
                                  jnp.where(row >= A_HEAD_DIM, qt, zero)], axis=1)
    q_pos = i * tq + lax.broadcasted_iota(I32, (1, tq), 1)
    q_chunk = jnp.concatenate([q_pos, q_pos], axis=1) >> 6
    m_sc[...] = jnp.full(m_sc.shape, NEG_INF, F32)
    acc_sc[...] = jnp.zeros(acc_sc.shape, F32)
    n_kv = ((i + 1) * tq + TILE_K - 1) // TILE_K

    def scores(j):
        return _dot(k_ref[_tile_rows(j), :], qq_sc[...])

    def update(t_ref, j):
        _flash_update_t(t_ref, m_sc, acc_sc, [vt_ref[VT_PER_K * j + c] for c in range(VT_PER_K)], tq)

    def masked_scores(j):
        return jnp.where(_chunk_mask_t(j, q_chunk), scores(j), NEG_INF)

    last = n_kv - 1
    ta_sc[...] = masked_scores(0)

    @pl.when(n_kv == 1)
    def _():
        update(ta_sc, 0)

    n_pairs = jnp.maximum(n_kv - 2, 0) // 2

    def pair(r, carry):
        j = 2 * r
        tb_sc[...] = scores(j + 1)
        update(ta_sc, j)
        ta_sc[...] = scores(j + 2)
        update(tb_sc, j + 1)
        return carry

    lax.fori_loop(0, n_pairs, pair, 0)
    j_cur = 2 * n_pairs

    @pl.when((n_kv > 1) & (last - j_cur == 1))
    def _():
        tb_sc[...] = masked_scores(last)
        update(ta_sc, j_cur)
        update(tb_sc, last)

    @pl.when((n_kv > 1) & (last - j_cur == 2))
    def _():
        tb_sc[...] = scores(j_cur + 1)
        update(ta_sc, j_cur)
        ta_sc[...] = masked_scores(last)
        update(tb_sc, j_cur + 1)
        update(ta_sc, last)

    acc = acc_sc[...]
    outs = [acc[0:LANES, mp * tq:(mp + 1) * tq] / acc[LANES:LANES + 1, mp * tq:(mp + 1) * tq] for mp in range(2)]
    o = outs[0] - lam_ref[0] * outs[1]
    y = o * lax.rsqrt(jnp.mean(o * o, axis=0, keepdims=True) + LN_EPS) * g_ref[...] * (1.0 - lam_init)
    o_ref[...] = y.T.astype(BF16)


def _diff_attn_prompt(lam, qat, ka, vat, g_col, lp, lam_init):
    n = ka.shape[0]
    return pl.pallas_call(
        functools.partial(_diff_p_kernel, lam_init=lam_init),
        grid=(A_HEADS, lp // TILE_Q_A),
        in_specs=[pl.BlockSpec(memory_space=pltpu.SMEM),
                  pl.BlockSpec((LANES, TILE_Q_A), lambda h, i: (h, i)),
                  pl.BlockSpec((lp, LANES), lambda h, i: (0, h)),
                  pl.BlockSpec((None, lp // TILE_ROWS, V_ROWS, TILE_ROWS), lambda h, i: (h, 0, 0, 0)),
                  pl.BlockSpec((LANES, 1), lambda h, i: (0, 0))],
        out_specs=pl.BlockSpec((TILE_Q_A, LANES), lambda h, i: (i, h)),
        out_shape=jax.ShapeDtypeStruct((n, A_HEADS * LANES), BF16),
        scratch_shapes=[pltpu.VMEM((LANES, 2 * TILE_Q_A), BF16),
                        pltpu.VMEM((TILE_K, 2 * TILE_Q_A), F32), pltpu.VMEM((TILE_K, 2 * TILE_Q_A), F32),
                        pltpu.VMEM((1, 2 * TILE_Q_A), F32), pltpu.VMEM((V_ROWS, 2 * TILE_Q_A), F32)],
        compiler_params=_params("parallel", "parallel"), name="diff_attn_prompt")(lam, qat, ka, vat, g_col)


def _diff_s_kernel(lam_ref, q_ref, ck_ref, cv_ref, kn_ref, vn_ref, g_ref, prev_ref, o_ref, *, lam_init, n_keys):
    del prev_ref
    ds_ = q_ref.shape[0]
    past = ck_ref.shape[0]
    pad = n_keys - past - ds_
    lane = lax.broadcasted_iota(I32, (1, LANES), 1)
    valid = lax.broadcasted_iota(I32, (1, n_keys), 1) < past + ds_
    for h in range(A_HEADS):
        cols = slice(h * LANES, (h + 1) * LANES)
        q = q_ref[:, cols]
        zero = jnp.zeros_like(q)
        k = jnp.concatenate([ck_ref[:, cols].astype(BF16), kn_ref[:, cols], jnp.zeros((pad, LANES), BF16)], 0)
        v = jnp.concatenate([cv_ref[:, cols].astype(BF16), vn_ref[:, cols], jnp.zeros((pad, LANES), BF16)], 0)
        qz = jnp.concatenate([jnp.where(lane < A_HEAD_DIM, q, zero), jnp.where(lane >= A_HEAD_DIM, q, zero)], 0)
        s = jnp.where(valid, _dot_nt(qz, k) * (A_HEAD_DIM ** -0.5), NEG_INF)
        p = jnp.exp(s - jnp.max(s, axis=1, keepdims=True))
        o = _dot(p.astype(BF16), v) / jnp.sum(p, axis=1, keepdims=True)
        o_ref[:, cols] = _subln(o[0:ds_], o[ds_:2 * ds_], lam_ref[0], g_ref[...], lam_init).astype(BF16)


def _diff_attn_sample(lam, qa, cache_k, cache_v, ka, va, g, ya, layer, lp, lam_init):
    _, nb, past, width = cache_k.shape
    n = qa.shape[0]
    ds_ = (n - lp) // nb
    n_keys = -(-(past + ds_) // LANES) * LANES
    base = lp // ds_
    new = pl.BlockSpec((ds_, width), lambda b: (base + b, 0))
    cache = pl.BlockSpec((None, None, past, width), lambda b: (layer, b, 0, 0))
    return pl.pallas_call(
        functools.partial(_diff_s_kernel, lam_init=lam_init, n_keys=n_keys), grid=(nb,),
        in_specs=[pl.BlockSpec(memory_space=pltpu.SMEM), new, cache, cache, new, new,
                  pl.BlockSpec((1, LANES), lambda b: (0, 0)), pl.BlockSpec(memory_space=pl.ANY)],
        out_specs=new, out_shape=jax.ShapeDtypeStruct(ya.shape, ya.dtype),
        input_output_aliases={7: 0},
        compiler_params=_params("parallel"), name="diff_attn_sample")(lam, qa, cache_k, cache_v, ka, va, g, ya)


def _index_scores(qiz, w_cols, ki4, rows):
    r = _dot_nt(qiz, ki4)
    sc = w_cols[0] * jnp.maximum(r[0:rows], 0.0)
    for h in range(1, IDX_HEADS):
        sc = sc + w_cols[h] * jnp.maximum(r[h * rows:(h + 1) * rows], 0.0)
    return sc


def _stack_idx_queries(qi, rows):
    lane = lax.broadcasted_iota(I32, (1, IDX_HEADS * IDX_DIM), 1)
    zero = jnp.zeros_like(qi)
    return [jnp.where((lane >= IDX_DIM * h) & (lane < IDX_DIM * (h + 1)), qi, zero) for h in range(IDX_HEADS)]


def _selected(key, t, need, seen, u):
    eq = key == t
    prefix = _dot(jnp.where(eq, 1.0, 0.0).astype(BF16), u)
    sel = ((key > t) | (eq & (seen + prefix <= need))) & (key > KEY_HALF_NEG)
    return sel, seen + prefix[:, -1:]


def _dsa_p_kernel(qbt_ref, qit_ref, wt_ref, kb_ref, vbt_ref, ki_ref, tri_ref, o_ref,
                  key_sc, qq_sc, ta_sc, tb_sc, sel_sc, m_sc, acc_sc, *, n_sel):
    tq = qbt_ref.shape[1]
    i = pl.program_id(0)
    n_kv = ((i + 1) * tq + TILE_K - 1) // TILE_K
    w = wt_ref[...] * (IDX_DIM ** -0.5)
    w_rows = [w[h:h + 1] for h in range(IDX_HEADS)]
    q_chunk = (i * tq + lax.broadcasted_iota(I32, (1, tq), 1)) >> 6

    last = n_kv - 1

    def put_products(t_ref, j):
        ki = ki_ref[_tile_rows(j), :]
        for h in range(IDX_HEADS):
            t_ref[:, h * tq:(h + 1) * tq] = _dot(ki, qit_ref[h * IDX_DIM:(h + 1) * IDX_DIM, :])

    def put_keys(t_ref, j, top):
        sc = None
        for h in range(IDX_HEADS):
            r = w_rows[h] * jnp.maximum(t_ref[:, h * tq:(h + 1) * tq], 0.0)
            sc = r if sc is None else sc + r
        key = _sort_key(jnp.where(_chunk_mask_t(j, q_chunk), sc, NEG_INF))
        key_sc[j] = key
        return jnp.maximum(top, jnp.max(key, axis=0, keepdims=True))

    put_products(ta_sc, 0)

    def score_pair(r, top):
        j = 2 * r
        j1 = jnp.minimum(j + 1, last)
        put_products(tb_sc, j1)
        top = put_keys(ta_sc, j, top)
        put_products(ta_sc, jnp.minimum(j + 2, last))
        return put_keys(tb_sc, j1, top)

    top_key = lax.fori_loop(0, (n_kv + 1) // 2, score_pair, jnp.full((1, tq), INT_MIN, I32))

    def count_ge(c):
        def body(j, acc):
            parts = [None] * COUNT_CHAINS
            for n, r in enumerate(range(0, TILE_K, COUNT_ROWS)):
                hit = jnp.where(key_sc[j, r:r + COUNT_ROWS, :] >= c, 1.0, 0.0)
                parts[n % COUNT_CHAINS] = hit if parts[n % COUNT_CHAINS] is None else parts[n % COUNT_CHAINS] + hit
            return acc + functools.reduce(lambda a, b: a + b, parts)
        return jnp.sum(lax.fori_loop(0, n_kv, body, jnp.zeros((COUNT_ROWS, tq), F32)), axis=0, keepdims=True)

    thr = _kth_largest_key_bracketed(count_ge, top_key, float(n_sel))
    need = float(n_sel) - count_ge(thr + 1)
    need = jnp.where(thr > KEY_HALF_NEG, need, 0.0)
    floor_key = jnp.maximum(thr, KEY_HALF_NEG)

    for c in range(B_KV_HEADS):
        for g in range(B_GROUP):
            hh = c * B_GROUP + g
            qq_sc[c, :, g * tq:(g + 1) * tq] = qbt_ref[hh * LANES:(hh + 1) * LANES, :]
    m_sc[...] = jnp.full(m_sc.shape, NEG_INF, F32)
    acc_sc[...] = jnp.zeros(acc_sc.shape, F32)

    def put_scores(t_ref, j, c):
        s = _dot(kb_ref[_tile_rows(j), c * LANES:(c + 1) * LANES], qq_sc[c])
        sel = sel_sc[...] > 0.5
        for g in range(B_GROUP):
            t_ref[:, g * tq:(g + 1) * tq] = jnp.where(sel, s[:, g * tq:(g + 1) * tq], NEG_INF)

    def update(t_ref, j, c):
        _flash_update_t(t_ref, m_sc.at[c], acc_sc.at[c],
                        [vbt_ref[c, VT_PER_K * j + cc] for cc in range(VT_PER_K)], tq)

    def select(j, seen):
        key = key_sc[j]
        tie = key == thr
        half = TILE_K // 2
        ones = jnp.where(tie, 1.0, 0.0).astype(BF16)
        for r in range(2):
            rows = slice(r * half, (r + 1) * half)
            prefix = seen + _dot(tri_ref[...], ones[rows])
            sel_sc[rows, :] = jnp.where((key[rows] > floor_key) | (tie[rows] & (prefix <= need)), 1.0, 0.0)
            seen = prefix[half - 1:half, :]
        return seen

    seen0 = select(0, jnp.zeros((1, tq), F32))
    put_scores(ta_sc, 0, 0)

    def attend(j, seen):
        nxt = jnp.minimum(j + 1, n_kv - 1)
        put_scores(tb_sc, j, 1)
        update(ta_sc, j, 0)
        seen = select(nxt, seen)
        put_scores(ta_sc, nxt, 0)
        update(tb_sc, j, 1)
        return seen

    lax.fori_loop(0, n_kv, attend, seen0)
    for c in range(B_KV_HEADS):
        acc = acc_sc[c]
        for g in range(B_GROUP):
            hh = c * B_GROUP + g
            a = acc[:, g * tq:(g + 1) * tq]
            o_ref[:, hh * LANES:(hh + 1) * LANES] = (a[0:LANES] / a[LANES:LANES + 1]).T.astype(BF16)


def _dsa_prompt(qbt, qit, wt, kb, vbt, ki, tri, lp, n_sel):
    n = kb.shape[0]
    tq = TILE_Q_B

    def col(height):
        return pl.BlockSpec((height, tq), lambda i: (0, i))

    def resident(shape):
        return pl.BlockSpec(shape, lambda i: (0,) * len(shape), pipeline_mode=pl.Buffered(1))

    return pl.pallas_call(
        functools.partial(_dsa_p_kernel, n_sel=n_sel), grid=(lp // tq,),
        in_specs=[col(1024), col(256), col(8), resident((lp, 256)),
                  resident((B_KV_HEADS, lp // TILE_ROWS, V_ROWS, TILE_ROWS)), resident((lp, IDX_DIM)),
                  resident((TILE_K // 2, TILE_K // 2))],
        out_specs=pl.BlockSpec((tq, 1024), lambda i: (i, 0)),
        out_shape=jax.ShapeDtypeStruct((n, 1024), BF16),
        scratch_shapes=[pltpu.VMEM((lp // TILE_K, TILE_K, tq), I32),
                        pltpu.VMEM((B_KV_HEADS, LANES, B_GROUP * tq), BF16),
                        pltpu.VMEM((TILE_K, B_GROUP * tq), F32), pltpu.VMEM((TILE_K, B_GROUP * tq), F32),
                        pltpu.VMEM((TILE_K, tq), F32),
                        pltpu.VMEM((B_KV_HEADS, 1, B_GROUP * tq), F32),
                        pltpu.VMEM((B_KV_HEADS, V_ROWS, B_GROUP * tq), F32)],
        compiler_params=_params("parallel"), name="dsa_prompt")(qbt, qit, wt, kb, vbt, ki, tri)


def _dsa_s_kernel(qb_ref, qi_ref, kiwi_ref, ckb_ref, cvb_ref, cki_ref, kbn_ref, vbn_ref, ki4n_ref,
                  rep_ref, u_ref, prev_ref, o_ref, *, n_sel, n_keys):
    del prev_ref
    tq = qb_ref.shape[0]
    past = ckb_ref.shape[0]
    pad = n_keys - past - tq
    width = B_KV_HEADS * B_HEAD_DIM
    kb = jnp.concatenate([ckb_ref[...].astype(BF16), kbn_ref[...], jnp.zeros((pad, width), BF16)], 0)
    vb = jnp.concatenate([cvb_ref[...].astype(BF16), vbn_ref[...], jnp.zeros((pad, width), BF16)], 0)
    ki4_past = _dot(cki_ref[...].astype(BF16), rep_ref[...]).astype(BF16)
    ki4 = jnp.concatenate([ki4_past, ki4n_ref[...], jnp.zeros((pad, width), BF16)], 0)
    qiz = jnp.concatenate(_stack_idx_queries(qi_ref[...], tq), 0)
    wq = kiwi_ref[...] * (IDX_DIM ** -0.5)
    w_cols = [wq[:, IDX_DIM + h:IDX_DIM + h + 1] for h in range(IDX_HEADS)]
    sc = _index_scores(qiz, w_cols, ki4, tq)
    valid = lax.broadcasted_iota(I32, (1, n_keys), 1) < past + tq
    key = _sort_key(jnp.where(valid, sc, NEG_INF))

    def count_ge(c):
        return jnp.sum(jnp.where(key >= c, 1.0, 0.0), axis=1, keepdims=True)

    t = _kth_largest_key(count_ge, (tq, 1), float(n_sel))
    need = float(n_sel) - count_ge(t + 1)
    sel, _ = _selected(key, t, need, jnp.zeros((tq, 1), F32), u_ref[...])
    for c in range(B_KV_HEADS):
        cols = slice(c * LANES, (c + 1) * LANES)
        q = jnp.concatenate([qb_ref[:, (c * B_GROUP + g) * LANES:(c * B_GROUP + g + 1) * LANES]
                             for g in range(B_GROUP)], 0)
        s_all = _dot_nt(q, kb[:, cols]) * (B_HEAD_DIM ** -0.5)
        for g in range(B_GROUP):
            hh = c * B_GROUP + g
            s = jnp.where(sel, s_all[g * tq:(g + 1) * tq], NEG_INF)
            p = jnp.exp(s - jnp.max(s, axis=1, keepdims=True))
            o = _dot(p.astype(BF16), vb[:, cols]) / jnp.sum(p, axis=1, keepdims=True)
            o_ref[:, hh * LANES:(hh + 1) * LANES] = o.astype(BF16)


def _dsa_sample(qb, qi, kiwi, cache_k, cache_v, cache_idx, kb, vb, ki4, rep, u, yb, layer, lp, n_sel):
    _, nb, past, _ = cache_k.shape
    n = qb.shape[0]
    ds_ = (n - lp) // nb
    n_keys = u.shape[0]
    base = lp // ds_

    def new(width):
        return pl.BlockSpec((ds_, width), lambda b: (base + b, 0))

    def cache(width):
        return pl.BlockSpec((None, None, past, width), lambda b: (layer, b, 0, 0))

    def const(shape):
        return pl.BlockSpec(shape, lambda b: (0, 0))

    return pl.pallas_call(
        functools.partial(_dsa_s_kernel, n_sel=n_sel, n_keys=n_keys), grid=(nb,),
        in_specs=[new(1024), new(256), new(LANES), cache(256), cache(256), cache(IDX_DIM),
                  new(256), new(256), new(256), const(rep.shape), const(u.shape),
                  pl.BlockSpec(memory_space=pl.ANY)],
        out_specs=new(1024), out_shape=jax.ShapeDtypeStruct(yb.shape, yb.dtype),
        input_output_aliases={11: 0},
        compiler_params=_params("parallel"), name="dsa_sample")(
            qb, qi, kiwi, cache_k, cache_v, cache_idx, kb, vb, ki4, rep, u, yb)


def _merge_kernel(h_ref, ya_ref, yb_ref, wga_ref, wgb_ref, wpa_ref, wpb_ref, wo_ref, g_ref, b_ref,
                  wr_ref, bias_ref, o_ref, gates_ref):
    h = h_ref[...]
    hb = h.astype(BF16)
    mixed = (jax.nn.sigmoid(_dot(hb, wga_ref[...])) * _dot(ya_ref[...], wpa_ref[...])
             + jax.nn.sigmoid(_dot(hb, wgb_ref[...])) * _dot(yb_ref[...], wpb_ref[...]))
    y = _dot(mixed.astype(BF16), wo_ref[...])
    h1 = _layernorm(DEEPNORM_ALPHA * h + y, g_ref[...], b_ref[...])
    o_ref[...] = h1
    s = jax.nn.sigmoid(_dot(h1.astype(BF16), wr_ref[...]))
    for r in range(0, s.shape[0], ROUTE_ROWS):
        gates_ref[r:r + ROUTE_ROWS, :] = _route(s[r:r + ROUTE_ROWS], bias_ref[...])


def _merge(h, ya, yb, wga, wgb, wpa, wpb, wo, g, b, wr, bias):
    n, d = h.shape
    row = pl.BlockSpec((TILE_ROWS, d), lambda i: (i, 0))
    mat = pl.BlockSpec((d, d), lambda i: (0, 0))
    vec = pl.BlockSpec((1, d), lambda i: (0, 0))
    return pl.pallas_call(
        _merge_kernel, grid=(n // TILE_ROWS,),
        in_specs=[row, row, row] + [mat] * 5 + [vec, vec, pl.BlockSpec((d, N_EXPERTS), lambda i: (0, 0)),
                                                 pl.BlockSpec((1, N_EXPERTS), lambda i: (0, 0))],
        out_specs=[row, pl.BlockSpec((TILE_ROWS, N_EXPERTS), lambda i: (i, 0))],
        out_shape=[jax.ShapeDtypeStruct((n, d), F32), jax.ShapeDtypeStruct((n, N_EXPERTS), F32)],
        compiler_params=_params("parallel"), name="merge_ln1_route")(
            h, ya, yb, wga, wgb, wpa, wpb, wo, g.reshape(1, d), b.reshape(1, d), wr,
            bias.reshape(1, N_EXPERTS))


def _route(s, bias):
    sel = s + bias
    lane = lax.broadcasted_iota(I32, (1, N_EXPERTS), 1)
    lane_f = lane.astype(F32)
    group = lane >> 3
    neg = -jnp.inf

    def first_argmax(x, m):
        return jnp.min(jnp.where(x == m, lane_f, float(N_EXPERTS)), axis=1, keepdims=True)

    g_score = []
    for g in range(N_GROUPS):
        x = jnp.where(group == g, sel, neg)
        m1 = jnp.max(x, axis=1, keepdims=True)
        x = jnp.where(lane_f == first_argmax(x, m1), neg, x)
        g_score.append(m1 + jnp.max(x, axis=1, keepdims=True))
    e_keep = None
    for g in range(N_GROUPS):
        ahead = jnp.zeros_like(g_score[g])
        for o in range(N_GROUPS):
            if o != g:
                beats = (g_score[o] > g_score[g]) | ((g_score[o] == g_score[g]) & (o < g))
                ahead = ahead + jnp.where(beats, 1.0, 0.0)
        keep = (group == g) & (ahead < float(TOPK_GROUPS))
        e_keep = keep if e_keep is None else (e_keep | keep)
    x = jnp.where(e_keep, sel, NEG_INF)
    chosen = None
    for _ in range(TOP_K):
        hit = lane_f == first_argmax(x, jnp.max(x, axis=1, keepdims=True))
        chosen = hit if chosen is None else (chosen | hit)
        x = jnp.where(hit, neg, x)
    w = jnp.where(chosen, s, 0.0)
    return w / jnp.sum(w, axis=1, keepdims=True) * ROUTE_SCALE


def _silu(x):
    return x * jax.nn.sigmoid(x)


def _moe_kernel(h_ref, gates_ref, wg_ref, wu_ref, wd_ref, sg_ref, su_ref, sd_ref, g_ref, b_ref, o_ref,
                xb_sc, acc_sc):
    step = pl.program_id(1)

    @pl.when(step == 0)
    def _():
        xb = h_ref[...].astype(BF16)
        xb_sc[...] = xb
        a = _silu(_dot(xb, sg_ref[...])) * _dot(xb, su_ref[...])
        acc_sc[...] = _dot(a.astype(BF16), sd_ref[...])

    xb = xb_sc[...]
    gates = gates_ref[...]
    lane = lax.broadcasted_iota(I32, (1, N_EXPERTS), 1)
    acts = []
    for e in range(EXPERT_BLOCK):
        gate = jnp.sum(jnp.where(lane == step * EXPERT_BLOCK + e, gates, 0.0), axis=1, keepdims=True)
        a = _silu(_dot(xb, wg_ref[e])) * _dot(xb, wu_ref[e]) * gate
        acts.append(a.astype(BF16))
    wd = wd_ref[...]
    acc_sc[...] += _dot(jnp.concatenate(acts, axis=1), wd.reshape(wd.shape[0] * wd.shape[1], wd.shape[2]))

    @pl.when(step == pl.num_programs(1) - 1)
    def _():
        o_ref[...] = _layernorm(DEEPNORM_ALPHA * h_ref[...] + acc_sc[...], g_ref[...], b_ref[...])


def _moe(h, gates, wg, wu, wd, sg, su, sd, g, b):
    n, d = h.shape
    f = wg.shape[-1]
    row = pl.BlockSpec((TILE_MOE, d), lambda r, e: (r, 0))
    vec = pl.BlockSpec((1, d), lambda r, e: (0, 0))
    return pl.pallas_call(
        _moe_kernel, grid=(n // TILE_MOE, N_EXPERTS // EXPERT_BLOCK),
        in_specs=[row, pl.BlockSpec((TILE_MOE, N_EXPERTS), lambda r, e: (r, 0)),
                  pl.BlockSpec((EXPERT_BLOCK, d, f), lambda r, e: (e, 0, 0)),
                  pl.BlockSpec((EXPERT_BLOCK, d, f), lambda r, e: (e, 0, 0)),
                  pl.BlockSpec((EXPERT_BLOCK, f, d), lambda r, e: (e, 0, 0)),
                  pl.BlockSpec((d, sg.shape[1]), lambda r, e: (0, 0)),
                  pl.BlockSpec((d, sg.shape[1]), lambda r, e: (0, 0)),
                  pl.BlockSpec((sg.shape[1], d), lambda r, e: (0, 0)), vec, vec],
        out_specs=row, out_shape=jax.ShapeDtypeStruct((n, d), F32),
        scratch_shapes=[pltpu.VMEM((TILE_MOE, d), BF16), pltpu.VMEM((TILE_MOE, d), F32)],
        compiler_params=_params("parallel", "arbitrary"), name="moe_ln2")(
            h, gates, wg, wu, wd, sg, su, sd, g.reshape(1, d), b.reshape(1, d))


def _rope_tables(pos):
    def table(half, reps):
        inv = ROPE_THETA ** (-jnp.arange(half, dtype=F32) / half)
        ang = pos.astype(F32)[:, None] * inv[None, :]
        cos, sin = jnp.cos(ang), jnp.sin(ang)
        return (jnp.tile(jnp.concatenate([cos, cos], -1), (1, reps)),
                jnp.tile(jnp.concatenate([-sin, sin], -1), (1, reps)))
    c64, s64 = table(A_HEAD_DIM // 2, 2)
    c128, s128 = table(B_HEAD_DIM // 2, 1)
    return c64, s64, c128, s128


def kernel(x_prompt, x_sample, cache_a_k, cache_a_v, cache_b_k, cache_b_v, cache_b_idx, meta_tokens, ln_in_g, ln_in_b, w_in, lam_q1, lam_k1, lam_q2, lam_k2, subln_g, w_proj_a, w_proj_b, w_out, ln1_g, ln1_b, w_router, router_bias, w_expert_gate, w_expert_up, w_expert_down, w_shared_gate, w_shared_up, w_shared_down, ln2_g, ln2_b):
    batch, seq, d = x_prompt.shape
    nb, ds_, _ = x_sample.shape
    depth, _, past = cache_a_k.shape[:3]
    assert batch == 1 and d == D_MODEL and seq % TILE_K == 0 and ds_ == CHUNK
    lp = FRONT_PAD + N_META + seq
    ns = nb * ds_
    n = lp + ns
    assert n % TILE_MOE == 0 and past % 16 == 0
    n_sel_p = min(TOPK_MAX, seq // 4)
    n_sel_s = min(TOPK_MAX, (past + ds_) // 4)
    n_keys_s = -(-(past + ds_) // LANES) * LANES

    x_all = jnp.concatenate([jnp.zeros((FRONT_PAD, d), F32), meta_tokens.astype(F32), x_prompt[0],
                             x_sample.reshape(ns, d)], 0)
    pos = jnp.concatenate([jnp.maximum(jnp.arange(lp, dtype=jnp.int32) - FRONT_PAD, 0),
                           jnp.tile(past + jnp.arange(ds_, dtype=jnp.int32), nb)])
    tabs = _rope_tables(pos)
    tri_p = jnp.tril(jnp.ones((TILE_K // 2, TILE_K // 2), BF16))
    u_s = jnp.triu(jnp.ones((n_keys_s, n_keys_s), BF16))
    rep = jnp.tile(jnp.eye(IDX_DIM, dtype=BF16), (1, IDX_HEADS))
    ck_a = cache_a_k.reshape(depth, nb, past, -1)
    cv_a = cache_a_v.reshape(depth, nb, past, -1)
    ck_b = cache_b_k.reshape(depth, nb, past, -1)
    cv_b = cache_b_v.reshape(depth, nb, past, -1)

    h = _ln_rows(x_all, ln_in_g, ln_in_b)
    rows = []
    for l in range(depth):
        lam_init = 0.8 - 0.6 * math.exp(-0.3 * l)
        lam = (jnp.exp(jnp.sum(lam_q1[l].astype(F32) * lam_k1[l].astype(F32)))
               - jnp.exp(jnp.sum(lam_q2[l].astype(F32) * lam_k2[l].astype(F32))) + lam_init).reshape(1)
        w1 = jnp.concatenate([w_in[l][:, :N_RAW_PROJ], jnp.zeros((d, N_PROJ - N_RAW_PROJ), F32)], 1).astype(BF16)
        wga = w_in[l][:, N_RAW_PROJ:N_RAW_PROJ + d].astype(BF16)
        wgb = w_in[l][:, N_RAW_PROJ + d:].astype(BF16)
        (qa, qat, ka_f, ka, va_f, va, vat, qb, qbt, kb_f, kb, vb_f, vb, vbt, qi, qit, kiwi, ki, ki4,
         wt) = _project(h, w1, tabs)
        rows.append((ka_f, va_f, kb_f, vb_f, kiwi))
        g = subln_g[l].astype(F32)
        ya = _diff_attn_prompt(lam, qat, ka, vat, g.reshape(LANES, 1), lp, lam_init)
        ya = _diff_attn_sample(lam, qa, ck_a, cv_a, ka, va, g.reshape(1, LANES), ya, l, lp, lam_init)
        yb = _dsa_prompt(qbt, qit, wt, kb, vbt, ki, tri_p, lp, n_sel_p)
        yb = _dsa_sample(qb, qi, kiwi, ck_b, cv_b, cache_b_idx, kb, vb, ki4, rep, u_s, yb, l, lp, n_sel_s)
        h, gates = _merge(h, ya, yb, wga, wgb, w_proj_a[l].astype(BF16), w_proj_b[l].astype(BF16),
                          w_out[l].astype(BF16), ln1_g[l], ln1_b[l], w_router[l].astype(BF16),
                          router_bias[l].astype(F32))
        h = _moe(h, gates, w_expert_gate[l].astype(BF16), w_expert_up[l].astype(BF16),
                 w_expert_down[l].astype(BF16), w_shared_gate[l].astype(BF16), w_shared_up[l].astype(BF16),
                 w_shared_down[l].astype(BF16), ln2_g[l], ln2_b[l])

    p0 = FRONT_PAD
    y_prompt = h[p0 + N_META:lp].reshape(1, seq, d)
    y_sample = h[lp:].reshape(nb, ds_, d)

    def stack(idx, width, tail):
        return (jnp.stack([r[idx][p0:lp, :width] for r in rows]).reshape((depth, 1, lp - p0) + tail),
                jnp.stack([r[idx][lp:, :width] for r in rows]).reshape((depth, nb, ds_) + tail))

    ak_p, ak_s = stack(0, 1024, (2 * A_HEADS, A_HEAD_DIM))
    av_p, av_s = stack(1, 1024, (A_HEADS, 2 * A_HEAD_DIM))
    bk_p, bk_s = stack(2, 256, (B_KV_HEADS, B_HEAD_DIM))
    bv_p, bv_s = stack(3, 256, (B_KV_HEADS, B_HEAD_DIM))
    bi_p, bi_s = stack(4, IDX_DIM, (IDX_DIM,))
    return (y_prompt, y_sample, ak_p, av_p, bk_p, bv_p, bi_p, ak_s, av_s, bk_s, bv_s, bi_s)
```

```python
import functools
import math

import numpy as np
import jax
import jax.numpy as jnp
from jax import lax
from jax.experimental import pallas as pl
from jax.experimental.pallas import tpu as pltpu

F32 = jnp.float32
BF16 = jnp.bfloat16
I32 = jnp.int32

MODEL_DEPTH = 4
CHUNK = 64
N_META = 16
ROPE_THETA = 10000.0
LN_EPS = 1e-5
NEG_INF = -1e30
DEEPNORM_ALPHA = (2 * MODEL_DEPTH) ** 0.25
A_HEADS = 8
A_HEAD_DIM = 64
B_HEADS = 8
B_KV_HEADS = 2
B_GROUP = B_HEADS // B_KV_HEADS
B_HEAD_DIM = 128
IDX_HEADS = 4
IDX_DIM = 64
TOPK_MAX = 256
N_EXPERTS = 64
TOP_K = 8
N_GROUPS = 8
GROUP_SIZE = N_EXPERTS // N_GROUPS
TOPK_GROUPS = 4
ROUTE_SCALE = 2.5

LANES = 128
D_MODEL = 1024
N_PROJ = 4992
COL_QA, COL_KA, COL_VA, COL_QB, COL_KB, COL_VB, COL_QI, COL_KIWI = (
    0, 1024, 2048, 3072, 4096, 4352, 4608, 4864)
N_RAW_PROJ = 4932

TILE_K = 512
FRONT_PAD = TILE_K - N_META
TILE_Q_A = 512
TILE_Q_B = 256
TILE_ROWS = 256
TILE_MOE = 512
ROUTE_ROWS = 64
EXPERT_BLOCK = 8
VMEM_LIMIT = 56 * 1024 * 1024
BIG_CHUNK = 1 << 30
INT_MIN = -(2 ** 31)
LOG2E = 1.4426950408889634
V_ROWS = 144
VT_PER_K = TILE_K // TILE_ROWS
COUNT_ROWS = 16
COUNT_CHAINS = 4
SOFTMAX_ROWS = 128


def _f32_key(v):
    b = int(np.float32(v).view(np.int32))
    return b ^ ((b >> 31) & 0x7FFFFFFF)


KEY_HALF_NEG = _f32_key(0.5 * NEG_INF)


def _dot(a, b):
    return jnp.dot(a, b, preferred_element_type=F32)


def _dot_nt(a, b):
    return lax.dot_general(a, b, (((1,), (1,)), ((), ())), preferred_element_type=F32)


def _layernorm(x, g, b):
    mu = jnp.mean(x, -1, keepdims=True)
    xc = x - mu
    var = jnp.mean(xc * xc, -1, keepdims=True)
    return xc * lax.rsqrt(var + LN_EPS) * g + b


def _sort_key(x):
    bits = lax.bitcast_convert_type(x, I32)
    return bits ^ ((bits >> 31) & 0x7FFFFFFF)


def _kth_largest_key(count_ge, shape, k):
    def body(b, tu):
        cand = tu | jnp.left_shift(jnp.int32(1), 31 - b)
        c = count_ge(cand ^ INT_MIN)
        return jnp.where(c >= k, cand, tu)
    tu = lax.fori_loop(0, 32, body, jnp.zeros(shape, I32))
    return tu ^ INT_MIN


def _kth_largest_key_bracketed(count_ge, top_key, k):
    def flags(b):
        return jnp.where(b, 1.0, 0.0)

    def any_set(f):
        return jnp.max(f) > 0.0

    enough = count_ge(jnp.full_like(top_key, KEY_HALF_NEG + 1)) >= k
    top_tie = enough & (count_ge(top_key) >= k)
    lo = jnp.where(top_tie, top_key, INT_MIN)
    hi = top_key
    done = top_tie | ~enough

    def gallop_cond(s):
        return (s[0] < 5) & any_set(s[3])

    def gallop(s):
        g, lo, hi, want = s
        need_lo = want > 0.5
        step = jnp.left_shift(jnp.int32(1 << 23), 2 * g)
        cand = jnp.where((g < 4) & (top_key >= INT_MIN + step), top_key - step, INT_MIN)
        ok = count_ge(cand) >= k
        lo = jnp.where(need_lo & ok, cand, lo)
        hi = jnp.where(need_lo & ~ok, cand, hi)
        return g + 1, lo, hi, flags(need_lo & ~ok)

    _, lo, hi, _ = lax.while_loop(gallop_cond, gallop, (jnp.int32(0), lo, hi, flags(~done)))

    def bisect_cond(s):
        return any_set(s[2])

    def bisect(s):
        lo, hi, open_f = s
        open_ = open_f > 0.5
        mid = lo + lax.shift_right_logical(hi - lo, jnp.full_like(lo, 1))
        c = count_ge(mid)
        ge = c >= k
        lo = jnp.where(open_ & ge, mid, lo)
        hi = jnp.where(open_ & ~ge, mid, hi)
        return lo, hi, flags(open_ & (c != k) & (hi - lo != 1))

    lo, _, _ = lax.while_loop(bisect_cond, bisect, (lo, hi, flags(~done & (hi - lo != 1))))
    return lo


def _tile_rows(j):
    return pl.ds(pl.multiple_of(j * TILE_K, TILE_K), TILE_K)


def _chunk_mask_t(j, q_chunk):
    k_idx = j * TILE_K + lax.broadcasted_iota(I32, (TILE_K, 1), 0)
    return jnp.where(k_idx >= FRONT_PAD, k_idx >> 6, BIG_CHUNK) <= q_chunk


def _flash_update_t(t_ref, m_ref, acc_ref, vt_tiles, width):
    chunks = [slice(r, r + SOFTMAX_ROWS) for r in range(0, TILE_K, SOFTMAX_ROWS)]
    for b in range(t_ref.shape[1] // width):
        cols = slice(b * width, (b + 1) * width)
        m_prev = m_ref[:, cols]
        m_new = m_prev
        for rows in chunks:
            m_new = jnp.maximum(m_new, jnp.max(t_ref[rows, cols], axis=0, keepdims=True))
        p = jnp.concatenate([jnp.exp2(t_ref[rows, cols] - m_new).astype(BF16) for rows in chunks], axis=0)
        pv = _dot(vt_tiles[0], p[0:TILE_ROWS])
        for c in range(1, VT_PER_K):
            pv = pv + _dot(vt_tiles[c], p[c * TILE_ROWS:(c + 1) * TILE_ROWS])
        acc_ref[:, cols] = jnp.exp2(m_prev - m_new) * acc_ref[:, cols] + pv
        m_ref[:, cols] = m_new


def _params(*sem):
    return pltpu.CompilerParams(dimension_semantics=sem, vmem_limit_bytes=VMEM_LIMIT)


def _ln_kernel(x_ref, g_ref, b_ref, o_ref):
    o_ref[...] = _layernorm(x_ref[...], g_ref[...], b_ref[...])


def _ln_rows(x, g, b):
    n, d = x.shape
    row = pl.BlockSpec((TILE_ROWS, d), lambda i: (i, 0))
    vec = pl.BlockSpec((1, d), lambda i: (0, 0))
    return pl.pallas_call(
        _ln_kernel, grid=(n // TILE_ROWS,), in_specs=[row, vec, vec], out_specs=row,
        out_shape=jax.ShapeDtypeStruct((n, d), F32), compiler_params=_params("parallel"),
        name="ln_in")(x, g.reshape(1, d), b.reshape(1, d))


def _proj_kernel(h_ref, w_ref, c64_ref, s64_ref, c128_ref, s128_ref,
                 qa_ref, qat_ref, kaf_ref, kab_ref, vaf_ref, vab_ref, vat_ref, qb_ref, qbt_ref,
                 kbf_ref, kbb_ref, vbf_ref, vbb_ref, vbt_ref, qi_ref, qit_ref, kiwi_ref, ki_ref,
                 ki4_ref, wt_ref):
    hb = h_ref[...].astype(BF16)
    c64, s64, c128, s128 = c64_ref[...], s64_ref[...], c128_ref[...], s128_ref[...]
    lane = lax.broadcasted_iota(I32, c64.shape, 1)
    low_half = (lane & (A_HEAD_DIM // 2)) == 0

    def rope64(y):
        swapped = jnp.where(low_half, pltpu.roll(y, LANES - 32, 1), pltpu.roll(y, 32, 1))
        return y * c64 + swapped * s64

    def rope128(y):
        return y * c128 + pltpu.roll(y, 64, 1) * s128

    def proj(col, width):
        return _dot(hb, w_ref[:, col:col + width])

    def tiles(y):
        return [y[:, t * LANES:(t + 1) * LANES] for t in range(y.shape[1] // LANES)]

    def put_values_t(vt_ref, y):
        for t, v in enumerate(tiles(y)):
            vt_ref[t, 0:LANES, :] = v.T.astype(BF16)
            vt_ref[t, LANES:V_ROWS, :] = jnp.ones((V_ROWS - LANES, v.shape[0]), BF16)

    for t, y in enumerate(tiles(proj(COL_QA, 1024))):
        r = rope64(y)
        qa_ref[:, t * LANES:(t + 1) * LANES] = r.astype(BF16)
        qat_ref[t * LANES:(t + 1) * LANES, :] = (r * (A_HEAD_DIM ** -0.5 * LOG2E)).T.astype(BF16)
    for t, y in enumerate(tiles(proj(COL_KA, 1024))):
        r = rope64(y)
        kaf_ref[:, t * LANES:(t + 1) * LANES] = r
        kab_ref[:, t * LANES:(t + 1) * LANES] = r.astype(BF16)
    y = proj(COL_VA, 1024)
    vaf_ref[...] = y
    vab_ref[...] = y.astype(BF16)
    put_values_t(vat_ref, y)
    for t, y in enumerate(tiles(proj(COL_QB, 1024))):
        r = rope128(y)
        qb_ref[:, t * LANES:(t + 1) * LANES] = r.astype(BF16)
        qbt_ref[t * LANES:(t + 1) * LANES, :] = (r * (B_HEAD_DIM ** -0.5 * LOG2E)).T.astype(BF16)
    for t, y in enumerate(tiles(proj(COL_KB, 256))):
        r = rope128(y)
        kbf_ref[:, t * LANES:(t + 1) * LANES] = r
        kbb_ref[:, t * LANES:(t + 1) * LANES] = r.astype(BF16)
    y = proj(COL_VB, 256)
    vbf_ref[...] = y
    vbb_ref[...] = y.astype(BF16)
    put_values_t(vbt_ref, y)
    for t, y in enumerate(tiles(proj(COL_QI, 256))):
        r = rope64(y)
        qi_ref[:, t * LANES:(t + 1) * LANES] = r.astype(BF16)
        qit_ref[t * LANES:(t + 1) * LANES, :] = r.T.astype(BF16)
    y = proj(COL_KIWI, LANES)
    is_key = lane < IDX_DIM
    kiwi = jnp.where(is_key, rope64(y), y * (IDX_HEADS ** -0.5))
    kiwi_ref[...] = kiwi
    ki_ref[...] = kiwi[:, 0:IDX_DIM].astype(BF16)
    ki2 = jnp.where(is_key, kiwi, pltpu.roll(kiwi, IDX_DIM, 1)).astype(BF16)
    ki4_ref[:, 0:LANES] = ki2
    ki4_ref[:, LANES:2 * LANES] = ki2
    wt_ref[...] = kiwi.T[IDX_DIM:IDX_DIM + 8, :]


def _project(h, w1, tabs):
    n = h.shape[0]

    def row(width):
        return pl.BlockSpec((TILE_ROWS, width), lambda i: (i, 0))

    def col(height):
        return (pl.BlockSpec((height, TILE_ROWS), lambda i: (0, i)), (height, n))

    def vals(heads):
        return (pl.BlockSpec((heads, None, V_ROWS, TILE_ROWS), lambda i: (0, i, 0, 0)),
                (heads, n // TILE_ROWS, V_ROWS, TILE_ROWS))

    def rows(width):
        return (row(width), (n, width))

    outs = [rows(1024) + (BF16,), col(1024) + (BF16,), rows(1024) + (F32,), rows(1024) + (BF16,),
            rows(1024) + (F32,), rows(1024) + (BF16,), vals(A_HEADS) + (BF16,),
            rows(1024) + (BF16,), col(1024) + (BF16,), rows(256) + (F32,), rows(256) + (BF16,),
            rows(256) + (F32,), rows(256) + (BF16,), vals(B_KV_HEADS) + (BF16,),
            rows(256) + (BF16,), col(256) + (BF16,), rows(LANES) + (F32,), rows(IDX_DIM) + (BF16,),
            rows(256) + (BF16,), col(8) + (F32,)]
    return pl.pallas_call(
        _proj_kernel, grid=(n // TILE_ROWS,),
        in_specs=[row(D_MODEL), pl.BlockSpec((D_MODEL, N_PROJ), lambda i: (0, 0))] + [row(LANES)] * 4,
        out_specs=[spec for spec, _, _ in outs],
        out_shape=[jax.ShapeDtypeStruct(shape, dt) for _, shape, dt in outs],
        compiler_params=_params("parallel"), name="in_proj")(h, w1, *tabs)


def _subln(o1, o2, lam, g, lam_init):
    o = o1 - lam * o2
    y = o * lax.rsqrt(jnp.mean(o * o, -1, keepdims=True) + LN_EPS) * g
    return y * (1.0 - lam_init)


def _diff_p_kernel(lam_ref, qt_ref, k_ref, vt_ref, g_ref, o_ref, qq_sc, ta_sc, tb_sc, m_sc, acc_sc, *,
                   lam_init):
    tq = qt_ref.shape[1]
    i = pl.program_id(1)
    qt = qt_ref[...]
    row = lax.broadcasted_iota(I32, (LANES, 1), 0)
    zero = jnp.zeros_like(qt)
    qq_sc[...] = jnp.concatenate([jnp.where(row < A_HEAD_DIM, qt, zero),
                                  jnp.where(row >= A_HEAD_DIM, qt, zero)], axis=1)
    q_pos = i * tq + lax.broadcasted_iota(I32, (1, tq), 1)
    q_chunk = jnp.concatenate([q_pos, q_pos], axis=1) >> 6
    m_sc[...] = jnp.full(m_sc.shape, NEG_INF, F32)
    acc_sc[...] = jnp.zeros(acc_sc.shape, F32)
    n_kv = ((i + 1) * tq + TILE_K - 1) // TILE_K

    def scores(j):
        return _dot(k_ref[_tile_rows(j), :], qq_sc[...])

    def update(t_ref, j):
        _flash_update_t(t_ref, m_sc, acc_sc, [vt_ref[VT_PER_K * j + c] for c in range(VT_PER_K)], tq)

    def masked_scores(j):
        return jnp.where(_chunk_mask_t(j, q_chunk), scores(j), NEG_INF)

    last = n_kv - 1
    ta_sc[...] = masked_scores(0)

    @pl.when(n_kv == 1)
    def _():
        update(ta_sc, 0)

    def pair(j):
        tb_sc[...] = scores(j + 1)
        update(ta_sc, j)
        ta_sc[...] = scores(j + 2)
        update(tb_sc, j + 1)

    n_quads = jnp.maximum(n_kv - 2, 0) // 4

    def quad(r, carry):
        pair(4 * r)
        pair(4 * r + 2)
        return carry

    lax.fori_loop(0, n_quads, quad, 0)
    extra_pair = (n_kv > 1) & (last - 4 * n_quads >= 3)

    @pl.when(extra_pair)
    def _():
        pair(4 * n_quads)

    j_cur = 4 * n_quads + 2 * extra_pair.astype(I32)

    @pl.when((n_kv > 1) & (last - j_cur == 1))
    def _():
        tb_sc[...] = masked_scores(last)
        update(ta_sc, j_cur)
        update(tb_sc, last)

    @pl.when((n_kv > 1) & (last - j_cur == 2))
    def _():
        tb_sc[...] = scores(j_cur + 1)
        update(ta_sc, j_cur)
        ta_sc[...] = masked_scores(last)
        update(tb_sc, j_cur + 1)
        update(ta_sc, last)

    acc = acc_sc[...]
    outs = [acc[0:LANES, mp * tq:(mp + 1) * tq] / acc[LANES:LANES + 1, mp * tq:(mp + 1) * tq] for mp in range(2)]
    o = outs[0] - lam_ref[0] * outs[1]
    y = o * lax.rsqrt(jnp.mean(o * o, axis=0, keepdims=True) + LN_EPS) * g_ref[...] * (1.0 - lam_init)
    o_ref[...] = y.T.astype(BF16)


def _diff_attn_prompt(lam, qat, ka, vat, g_col, lp, lam_init):
    n = ka.shape[0]
    return pl.pallas_call(
        functools.partial(_diff_p_kernel, lam_init=lam_init),
        grid=(A_HEADS, lp // TILE_Q_A),
        in_specs=[pl.BlockSpec(memory_space=pltpu.SMEM),
                  pl.BlockSpec((LANES, TILE_Q_A), lambda h, i: (h, i)),
                  pl.BlockSpec((lp, LANES), lambda h, i: (0, h)),
                  pl.BlockSpec((None, lp // TILE_ROWS, V_ROWS, TILE_ROWS), lambda h, i: (h, 0, 0, 0)),
                  pl.BlockSpec((LANES, 1), lambda h, i: (0, 0))],
        out_specs=pl.BlockSpec((TILE_Q_A, LANES), lambda h, i: (i, h)),
        out_shape=jax.ShapeDtypeStruct((n, A_HEADS * LANES), BF16),
        scratch_shapes=[pltpu.VMEM((LANES, 2 * TILE_Q_A), BF16),
                        pltpu.VMEM((TILE_K, 2 * TILE_Q_A), F32), pltpu.VMEM((TILE_K, 2 * TILE_Q_A), F32),
                        pltpu.VMEM((1, 2 * TILE_Q_A), F32), pltpu.VMEM((V_ROWS, 2 * TILE_Q_A), F32)],
        compiler_params=_params("parallel", "parallel"), name="diff_attn_prompt")(lam, qat, ka, vat, g_col)


def _diff_s_kernel(lam_ref, q_ref, ck_ref, cv_ref, kn_ref, vn_ref, g_ref, prev_ref, o_ref, *, lam_init, n_keys):
    del prev_ref
    ds_ = q_ref.shape[0]
    past = ck_ref.shape[0]
    pad = n_keys - past - ds_
    lane = lax.broadcasted_iota(I32, (1, LANES), 1)
    valid = lax.broadcasted_iota(I32, (1, n_keys), 1) < past + ds_
    for h in range(A_HEADS):
        cols = slice(h * LANES, (h + 1) * LANES)
        q = q_ref[:, cols]
        zero = jnp.zeros_like(q)
        k = jnp.concatenate([ck_ref[:, cols].astype(BF16), kn_ref[:, cols], jnp.zeros((pad, LANES), BF16)], 0)
        v = jnp.concatenate([cv_ref[:, cols].astype(BF16), vn_ref[:, cols], jnp.zeros((pad, LANES), BF16)], 0)
        qz = jnp.concatenate([jnp.where(lane < A_HEAD_DIM, q, zero), jnp.where(lane >= A_HEAD_DIM, q, zero)], 0)
        s = jnp.where(valid, _dot_nt(qz, k) * (A_HEAD_DIM ** -0.5), NEG_INF)
        p = jnp.exp(s - jnp.max(s, axis=1, keepdims=True))
        o = _dot(p.astype(BF16), v) / jnp.sum(p, axis=1, keepdims=True)
        o_ref[:, cols] = _subln(o[0:ds_], o[ds_:2 * ds_], lam_ref[0], g_ref[...], lam_init).astype(BF16)


def _diff_attn_sample(lam, qa, cache_k, cache_v, ka, va, g, ya, layer, lp, lam_init):
    _, nb, past, width = cache_k.shape
    n = qa.shape[0]
    ds_ = (n - lp) // nb
    n_keys = -(-(past + ds_) // LANES) * LANES
    base = lp // ds_
    new = pl.BlockSpec((ds_, width), lambda b: (base + b, 0))
    cache = pl.BlockSpec((None, None, past, width), lambda b: (layer, b, 0, 0))
    return pl.pallas_call(
        functools.partial(_diff_s_kernel, lam_init=lam_init, n_keys=n_keys), grid=(nb,),
        in_specs=[pl.BlockSpec(memory_space=pltpu.SMEM), new, cache, cache, new, new,
                  pl.BlockSpec((1, LANES), lambda b: (0, 0)), pl.BlockSpec(memory_space=pl.ANY)],
        out_specs=new, out_shape=jax.ShapeDtypeStruct(ya.shape, ya.dtype),
        input_output_aliases={7: 0},
        compiler_params=_params("parallel"), name="diff_attn_sample")(lam, qa, cache_k, cache_v, ka, va, g, ya)


def _index_scores(qiz, w_cols, ki4, rows):
    r = _dot_nt(qiz, ki4)
    sc = w_cols[0] * jnp.maximum(r[0:rows], 0.0)
    for h in range(1, IDX_HEADS):
        sc = sc + w_cols[h] * jnp.maximum(r[h * rows:(h + 1) * rows], 0.0)
    return sc


def _stack_idx_queries(qi, rows):
    lane = lax.broadcasted_iota(I32, (1, IDX_HEADS * IDX_DIM), 1)
    zero = jnp.zeros_like(qi)
    return [jnp.where((lane >= IDX_DIM * h) & (lane < IDX_DIM * (h + 1)), qi, zero) for h in range(IDX_HEADS)]


def _selected(key, t, need, seen, u):
    eq = key == t
    prefix = _dot(jnp.where(eq, 1.0, 0.0).astype(BF16), u)
    sel = ((key > t) | (eq & (seen + prefix <= need))) & (key > KEY_HALF_NEG)
    return sel, seen + prefix[:, -1:]


def _dsa_p_kernel(qbt_ref, qit_ref, wt_ref, kb_ref, vbt_ref, ki_ref, tri_ref, o_ref,
                  key_sc, qq_sc, ta_sc, tb_sc, sel_sc, m_sc, acc_sc, *, n_sel):
    tq = qbt_ref.shape[1]
    i = pl.program_id(0)
    n_kv = ((i + 1) * tq + TILE_K - 1) // TILE_K
    w = wt_ref[...] * (IDX_DIM ** -0.5)
    w_rows = [w[h:h + 1] for h in range(IDX_HEADS)]
    q_chunk = (i * tq + lax.broadcasted_iota(I32, (1, tq), 1)) >> 6

    last = n_kv - 1

    def put_products(t_ref, j):
        ki = ki_ref[_tile_rows(j), :]
        for h in range(IDX_HEADS):
            t_ref[:, h * tq:(h + 1) * tq] = _dot(ki, qit_ref[h * IDX_DIM:(h + 1) * IDX_DIM, :])

    def put_keys(t_ref, j, top):
        sc = None
        for h in range(IDX_HEADS):
            r = w_rows[h] * jnp.maximum(t_ref[:, h * tq:(h + 1) * tq], 0.0)
            sc = r if sc is None else sc + r
        key = _sort_key(jnp.where(_chunk_mask_t(j, q_chunk), sc, NEG_INF))
        key_sc[j] = key
        return jnp.maximum(top, jnp.max(key, axis=0, keepdims=True))

    put_products(ta_sc, 0)

    def score_pair(r, top):
        j = 2 * r
        j1 = jnp.minimum(j + 1, last)
        put_products(tb_sc, j1)
        top = put_keys(ta_sc, j, top)
        put_products(ta_sc, jnp.minimum(j + 2, last))
        return put_keys(tb_sc, j1, top)

    top_key = lax.fori_loop(0, (n_kv + 1) // 2, score_pair, jnp.full((1, tq), INT_MIN, I32))

    def count_ge(c):
        def body(j, acc):
            parts = [None] * COUNT_CHAINS
            for n, r in enumerate(range(0, TILE_K, COUNT_ROWS)):
                hit = jnp.where(key_sc[j, r:r + COUNT_ROWS, :] >= c, 1.0, 0.0)
                parts[n % COUNT_CHAINS] = hit if parts[n % COUNT_CHAINS] is None else parts[n % COUNT_CHAINS] + hit
            return acc + functools.reduce(lambda a, b: a + b, parts)
        return jnp.sum(lax.fori_loop(0, n_kv, body, jnp.zeros((COUNT_ROWS, tq), F32)), axis=0, keepdims=True)

    thr = _kth_largest_key_bracketed(count_ge, top_key, float(n_sel))
    need = float(n_sel) - count_ge(thr + 1)
    need = jnp.where(thr > KEY_HALF_NEG, need, 0.0)
    floor_key = jnp.maximum(thr, KEY_HALF_NEG)

    for c in range(B_KV_HEADS):
        for g in range(B_GROUP):
            hh = c * B_GROUP + g
            qq_sc[c, :, g * tq:(g + 1) * tq] = qbt_ref[hh * LANES:(hh + 1) * LANES, :]
    m_sc[...] = jnp.full(m_sc.shape, NEG_INF, F32)
    acc_sc[...] = jnp.zeros(acc_sc.shape, F32)

    def put_scores(t_ref, j, c):
        s = _dot(kb_ref[_tile_rows(j), c * LANES:(c + 1) * LANES], qq_sc[c])
        sel = sel_sc[...] > 0.5
        for g in range(B_GROUP):
            t_ref[:, g * tq:(g + 1) * tq] = jnp.where(sel, s[:, g * tq:(g + 1) * tq], NEG_INF)

    def update(t_ref, j, c):
        _flash_update_t(t_ref, m_sc.at[c], acc_sc.at[c],
                        [vbt_ref[c, VT_PER_K * j + cc] for cc in range(VT_PER_K)], tq)

    def select(j, seen):
        key = key_sc[j]
        tie = key == thr
        half = TILE_K // 2
        ones = jnp.where(tie, 1.0, 0.0).astype(BF16)
        for r in range(2):
            rows = slice(r * half, (r + 1) * half)
            prefix = seen + _dot(tri_ref[...], ones[rows])
            sel_sc[rows, :] = jnp.where((key[rows] > floor_key) | (tie[rows] & (prefix <= need)), 1.0, 0.0)
            seen = prefix[half - 1:half, :]
        return seen

    seen0 = select(0, jnp.zeros((1, tq), F32))
    put_scores(ta_sc, 0, 0)

    def attend(j, seen):
        nxt = jnp.minimum(j + 1, last)
        put_scores(tb_sc, j, 1)
        update(ta_sc, j, 0)
        seen = select(nxt, seen)
        put_scores(ta_sc, nxt, 0)
        update(tb_sc, j, 1)
        return seen

    def attend_two(r, seen):
        return attend(2 * r + 1, attend(2 * r, seen))

    seen_end = lax.fori_loop(0, n_kv // 2, attend_two, seen0)

    @pl.when(n_kv % 2 == 1)
    def _():
        attend(last, seen_end)
    for c in range(B_KV_HEADS):
        acc = acc_sc[c]
        for g in range(B_GROUP):
            hh = c * B_GROUP + g
            a = acc[:, g * tq:(g + 1) * tq]
            o_ref[:, hh * LANES:(hh + 1) * LANES] = (a[0:LANES] / a[LANES:LANES + 1]).T.astype(BF16)


def _dsa_prompt(qbt, qit, wt, kb, vbt, ki, tri, lp, n_sel):
    n = kb.shape[0]
    tq = TILE_Q_B

    def col(height):
        return pl.BlockSpec((height, tq), lambda i: (0, i))

    def resident(shape):
        return pl.BlockSpec(shape, lambda i: (0,) * len(shape), pipeline_mode=pl.Buffered(1))

    return pl.pallas_call(
        functools.partial(_dsa_p_kernel, n_sel=n_sel), grid=(lp // tq,),
        in_specs=[col(1024), col(256), col(8), resident((lp, 256)),
                  resident((B_KV_HEADS, lp // TILE_ROWS, V_ROWS, TILE_ROWS)), resident((lp, IDX_DIM)),
                  resident((TILE_K // 2, TILE_K // 2))],
        out_specs=pl.BlockSpec((tq, 1024), lambda i: (i, 0)),
        out_shape=jax.ShapeDtypeStruct((n, 1024), BF16),
        scratch_shapes=[pltpu.VMEM((lp // TILE_K, TILE_K, tq), I32),
                        pltpu.VMEM((B_KV_HEADS, LANES, B_GROUP * tq), BF16),
                        pltpu.VMEM((TILE_K, B_GROUP * tq), F32), pltpu.VMEM((TILE_K, B_GROUP * tq), F32),
                        pltpu.VMEM((TILE_K, tq), F32),
                        pltpu.VMEM((B_KV_HEADS, 1, B_GROUP * tq), F32),
                        pltpu.VMEM((B_KV_HEADS, V_ROWS, B_GROUP * tq), F32)],
        compiler_params=_params("parallel"), name="dsa_prompt")(qbt, qit, wt, kb, vbt, ki, tri)


def _dsa_s_kernel(qb_ref, qi_ref, kiwi_ref, ckb_ref, cvb_ref, cki_ref, kbn_ref, vbn_ref, ki4n_ref,
                  rep_ref, u_ref, prev_ref, o_ref, *, n_sel, n_keys):
    del prev_ref
    tq = qb_ref.shape[0]
    past = ckb_ref.shape[0]
    pad = n_keys - past - tq
    width = B_KV_HEADS * B_HEAD_DIM
    kb = jnp.concatenate([ckb_ref[...].astype(BF16), kbn_ref[...], jnp.zeros((pad, width), BF16)], 0)
    vb = jnp.concatenate([cvb_ref[...].astype(BF16), vbn_ref[...], jnp.zeros((pad, width), BF16)], 0)
    ki4_past = _dot(cki_ref[...].astype(BF16), rep_ref[...]).astype(BF16)
    ki4 = jnp.concatenate([ki4_past, ki4n_ref[...], jnp.zeros((pad, width), BF16)], 0)
    qiz = jnp.concatenate(_stack_idx_queries(qi_ref[...], tq), 0)
    wq = kiwi_ref[...] * (IDX_DIM ** -0.5)
    w_cols = [wq[:, IDX_DIM + h:IDX_DIM + h + 1] for h in range(IDX_HEADS)]
    sc = _index_scores(qiz, w_cols, ki4, tq)
    valid = lax.broadcasted_iota(I32, (1, n_keys), 1) < past + tq
    key = _sort_key(jnp.where(valid, sc, NEG_INF))

    def count_ge(c):
        return jnp.sum(jnp.where(key >= c, 1.0, 0.0), axis=1, keepdims=True)

    t = _kth_largest_key(count_ge, (tq, 1), float(n_sel))
    need = float(n_sel) - count_ge(t + 1)
    sel, _ = _selected(key, t, need, jnp.zeros((tq, 1), F32), u_ref[...])
    for c in range(B_KV_HEADS):
        cols = slice(c * LANES, (c + 1) * LANES)
        q = jnp.concatenate([qb_ref[:, (c * B_GROUP + g) * LANES:(c * B_GROUP + g + 1) * LANES]
                             for g in range(B_GROUP)], 0)
        s_all = _dot_nt(q, kb[:, cols]) * (B_HEAD_DIM ** -0.5)
        for g in range(B_GROUP):
            hh = c * B_GROUP + g
            s = jnp.where(sel, s_all[g * tq:(g + 1) * tq], NEG_INF)
            p = jnp.exp(s - jnp.max(s, axis=1, keepdims=True))
            o = _dot(p.astype(BF16), vb[:, cols]) / jnp.sum(p, axis=1, keepdims=True)
            o_ref[:, hh * LANES:(hh + 1) * LANES] = o.astype(BF16)


def _dsa_sample(qb, qi, kiwi, cache_k, cache_v, cache_idx, kb, vb, ki4, rep, u, yb, layer, lp, n_sel):
    _, nb, past, _ = cache_k.shape
    n = qb.shape[0]
    ds_ = (n - lp) // nb
    n_keys = u.shape[0]
    base = lp // ds_

    def new(width):
        return pl.BlockSpec((ds_, width), lambda b: (base + b, 0))

    def cache(width):
        return pl.BlockSpec((None, None, past, width), lambda b: (layer, b, 0, 0))

    def const(shape):
        return pl.BlockSpec(shape, lambda b: (0, 0))

    return pl.pallas_call(
        functools.partial(_dsa_s_kernel, n_sel=n_sel, n_keys=n_keys), grid=(nb,),
        in_specs=[new(1024), new(256), new(LANES), cache(256), cache(256), cache(IDX_DIM),
                  new(256), new(256), new(256), const(rep.shape), const(u.shape),
                  pl.BlockSpec(memory_space=pl.ANY)],
        out_specs=new(1024), out_shape=jax.ShapeDtypeStruct(yb.shape, yb.dtype),
        input_output_aliases={11: 0},
        compiler_params=_params("parallel"), name="dsa_sample")(
            qb, qi, kiwi, cache_k, cache_v, cache_idx, kb, vb, ki4, rep, u, yb)


def _merge_kernel(h_ref, ya_ref, yb_ref, wga_ref, wgb_ref, wpa_ref, wpb_ref, wo_ref, g_ref, b_ref,
                  wr_ref, bias_ref, o_ref, gates_ref):
    h = h_ref[...]
    hb = h.astype(BF16)
    mixed = (jax.nn.sigmoid(_dot(hb, wga_ref[...])) * _dot(ya_ref[...], wpa_ref[...])
             + jax.nn.sigmoid(_dot(hb, wgb_ref[...])) * _dot(yb_ref[...], wpb_ref[...]))
    y = _dot(mixed.astype(BF16), wo_ref[...])
    h1 = _layernorm(DEEPNORM_ALPHA * h + y, g_ref[...], b_ref[...])
    o_ref[...] = h1
    s = jax.nn.sigmoid(_dot(h1.astype(BF16), wr_ref[...]))
    for r in range(0, s.shape[0], ROUTE_ROWS):
        gates_ref[r:r + ROUTE_ROWS, :] = _route(s[r:r + ROUTE_ROWS], bias_ref[...])


def _merge(h, ya, yb, wga, wgb, wpa, wpb, wo, g, b, wr, bias):
    n, d = h.shape
    row = pl.BlockSpec((TILE_ROWS, d), lambda i: (i, 0))
    mat = pl.BlockSpec((d, d), lambda i: (0, 0))
    vec = pl.BlockSpec((1, d), lambda i: (0, 0))
    return pl.pallas_call(
        _merge_kernel, grid=(n // TILE_ROWS,),
        in_specs=[row, row, row] + [mat] * 5 + [vec, vec, pl.BlockSpec((d, N_EXPERTS), lambda i: (0, 0)),
                                                 pl.BlockSpec((1, N_EXPERTS), lambda i: (0, 0))],
        out_specs=[row, pl.BlockSpec((TILE_ROWS, N_EXPERTS), lambda i: (i, 0))],
        out_shape=[jax.ShapeDtypeStruct((n, d), F32), jax.ShapeDtypeStruct((n, N_EXPERTS), F32)],
        compiler_params=_params("parallel"), name="merge_ln1_route")(
            h, ya, yb, wga, wgb, wpa, wpb, wo, g.reshape(1, d), b.reshape(1, d), wr,
            bias.reshape(1, N_EXPERTS))


def _route(s, bias):
    sel = s + bias
    lane = lax.broadcasted_iota(I32, (1, N_EXPERTS), 1)
    lane_f = lane.astype(F32)
    group = lane >> 3
    neg = -jnp.inf

    def first_argmax(x, m):
        return jnp.min(jnp.where(x == m, lane_f, float(N_EXPERTS)), axis=1, keepdims=True)

    g_score = []
    for g in range(N_GROUPS):
        x = jnp.where(group == g, sel, neg)
        m1 = jnp.max(x, axis=1, keepdims=True)
        x = jnp.where(lane_f == first_argmax(x, m1), neg, x)
        g_score.append(m1 + jnp.max(x, axis=1, keepdims=True))
    e_keep = None
    for g in range(N_GROUPS):
        ahead = jnp.zeros_like(g_score[g])
        for o in range(N_GROUPS):
            if o != g:
                beats = (g_score[o] > g_score[g]) | ((g_score[o] == g_score[g]) & (o < g))
                ahead = ahead + jnp.where(beats, 1.0, 0.0)
        keep = (group == g) & (ahead < float(TOPK_GROUPS))
        e_keep = keep if e_keep is None else (e_keep | keep)
    x = jnp.where(e_keep, sel, NEG_INF)
    chosen = None
    for _ in range(TOP_K):
        hit = lane_f == first_argmax(x, jnp.max(x, axis=1, keepdims=True))
        chosen = hit if chosen is None else (chosen | hit)
        x = jnp.where(hit, neg, x)
    w = jnp.where(chosen, s, 0.0)
    return w / jnp.sum(w, axis=1, keepdims=True) * ROUTE_SCALE


def _silu(x):
    return x * jax.nn.sigmoid(x)


def _moe_kernel(h_ref, gates_ref, wg_ref, wu_ref, wd_ref, sg_ref, su_ref, sd_ref, g_ref, b_ref, o_ref,
                xb_sc, acc_sc):
    step = pl.program_id(1)

    @pl.when(step == 0)
    def _():
        xb = h_ref[...].astype(BF16)
        xb_sc[...] = xb
        a = _silu(_dot(xb, sg_ref[...])) * _dot(xb, su_ref[...])
        acc_sc[...] = _dot(a.astype(BF16), sd_ref[...])

    xb = xb_sc[...]
    gates = gates_ref[...]
    lane = lax.broadcasted_iota(I32, (1, N_EXPERTS), 1)
    acts = []
    for e in range(EXPERT_BLOCK):
        gate = jnp.sum(jnp.where(lane == step * EXPERT_BLOCK + e, gates, 0.0), axis=1, keepdims=True)
        a = _silu(_dot(xb, wg_ref[e])) * _dot(xb, wu_ref[e]) * gate
        acts.append(a.astype(BF16))
    wd = wd_ref[...]
    acc_sc[...] += _dot(jnp.concatenate(acts, axis=1), wd.reshape(wd.shape[0] * wd.shape[1], wd.shape[2]))

    @pl.when(step == pl.num_programs(1) - 1)
    def _():
        o_ref[...] = _layernorm(DEEPNORM_ALPHA * h_ref[...] + acc_sc[...], g_ref[...], b_ref[...])


def _moe(h, gates, wg, wu, wd, sg, su, sd, g, b):
    n, d = h.shape
    f = wg.shape[-1]
    row = pl.BlockSpec((TILE_MOE, d), lambda r, e: (r, 0))
    vec = pl.BlockSpec((1, d), lambda r, e: (0, 0))
    return pl.pallas_call(
        _moe_kernel, grid=(n // TILE_MOE, N_EXPERTS // EXPERT_BLOCK),
        in_specs=[row, pl.BlockSpec((TILE_MOE, N_EXPERTS), lambda r, e: (r, 0)),
                  pl.BlockSpec((EXPERT_BLOCK, d, f), lambda r, e: (e, 0, 0)),
                  pl.BlockSpec((EXPERT_BLOCK, d, f), lambda r, e: (e, 0, 0)),
                  pl.BlockSpec((EXPERT_BLOCK, f, d), lambda r, e: (e, 0, 0)),
                  pl.BlockSpec((d, sg.shape[1]), lambda r, e: (0, 0)),
                  pl.BlockSpec((d, sg.shape[1]), lambda r, e: (0, 0)),
                  pl.BlockSpec((sg.shape[1], d), lambda r, e: (0, 0)), vec, vec],
        out_specs=row, out_shape=jax.ShapeDtypeStruct((n, d), F32),
        scratch_shapes=[pltpu.VMEM((TILE_MOE, d), BF16), pltpu.VMEM((TILE_MOE, d), F32)],
        compiler_params=_params("parallel", "arbitrary"), name="moe_ln2")(
            h, gates, wg, wu, wd, sg, su, sd, g.reshape(1, d), b.reshape(1, d))


def _rope_tables(pos):
    def table(half, reps):
        inv = ROPE_THETA ** (-jnp.arange(half, dtype=F32) / half)
        ang = pos.astype(F32)[:, None] * inv[None, :]
        cos, sin = jnp.cos(ang), jnp.sin(ang)
        return (jnp.tile(jnp.concatenate([cos, cos], -1), (1, reps)),
                jnp.tile(jnp.concatenate([-sin, sin], -1), (1, reps)))
    c64, s64 = table(A_HEAD_DIM // 2, 2)
    c128, s128 = table(B_HEAD_DIM // 2, 1)
    return c64, s64, c128, s128


def kernel(x_prompt, x_sample, cache_a_k, cache_a_v, cache_b_k, cache_b_v, cache_b_idx, meta_tokens, ln_in_g, ln_in_b, w_in, lam_q1, lam_k1, lam_q2, lam_k2, subln_g, w_proj_a, w_proj_b, w_out, ln1_g, ln1_b, w_router, router_bias, w_expert_gate, w_expert_up, w_expert_down, w_shared_gate, w_shared_up, w_shared_down, ln2_g, ln2_b):
    batch, seq, d = x_prompt.shape
    nb, ds_, _ = x_sample.shape
    depth, _, past = cache_a_k.shape[:3]
    assert batch == 1 and d == D_MODEL and seq % TILE_K == 0 and ds_ == CHUNK
    lp = FRONT_PAD + N_META + seq
    ns = nb * ds_
    n = lp + ns
    assert n % TILE_MOE == 0 and past % 16 == 0
    n_sel_p = min(TOPK_MAX, seq // 4)
    n_sel_s = min(TOPK_MAX, (past + ds_) // 4)
    n_keys_s = -(-(past + ds_) // LANES) * LANES

    x_all = jnp.concatenate([jnp.zeros((FRONT_PAD, d), F32), meta_tokens.astype(F32), x_prompt[0],
                             x_sample.reshape(ns, d)], 0)
    pos = jnp.concatenate([jnp.maximum(jnp.arange(lp, dtype=jnp.int32) - FRONT_PAD, 0),
                           jnp.tile(past + jnp.arange(ds_, dtype=jnp.int32), nb)])
    tabs = _rope_tables(pos)
    tri_p = jnp.tril(jnp.ones((TILE_K // 2, TILE_K // 2), BF16))
    u_s = jnp.triu(jnp.ones((n_keys_s, n_keys_s), BF16))
    rep = jnp.tile(jnp.eye(IDX_DIM, dtype=BF16), (1, IDX_HEADS))
    ck_a = cache_a_k.reshape(depth, nb, past, -1)
    cv_a = cache_a_v.reshape(depth, nb, past, -1)
    ck_b = cache_b_k.reshape(depth, nb, past, -1)
    cv_b = cache_b_v.reshape(depth, nb, past, -1)

    h = _ln_rows(x_all, ln_in_g, ln_in_b)
    rows = []
    for l in range(depth):
        lam_init = 0.8 - 0.6 * math.exp(-0.3 * l)
        lam = (jnp.exp(jnp.sum(lam_q1[l].astype(F32) * lam_k1[l].astype(F32)))
               - jnp.exp(jnp.sum(lam_q2[l].astype(F32) * lam_k2[l].astype(F32))) + lam_init).reshape(1)
        w1 = jnp.concatenate([w_in[l][:, :N_RAW_PROJ], jnp.zeros((d, N_PROJ - N_RAW_PROJ), F32)], 1).astype(BF16)
        wga = w_in[l][:, N_RAW_PROJ:N_RAW_PROJ + d].astype(BF16)
        wgb = w_in[l][:, N_RAW_PROJ + d:].astype(BF16)
        (qa, qat, ka_f, ka, va_f, va, vat, qb, qbt, kb_f, kb, vb_f, vb, vbt, qi, qit, kiwi, ki, ki4,
         wt) = _project(h, w1, tabs)
        rows.append((ka_f, va_f, kb_f, vb_f, kiwi))
        g = subln_g[l].astype(F32)
        ya = _diff_attn_prompt(lam, qat, ka, vat, g.reshape(LANES, 1), lp, lam_init)
        ya = _diff_attn_sample(lam, qa, ck_a, cv_a, ka, va, g.reshape(1, LANES), ya, l, lp, lam_init)
        yb = _dsa_prompt(qbt, qit, wt, kb, vbt, ki, tri_p, lp, n_sel_p)
        yb = _dsa_sample(qb, qi, kiwi, ck_b, cv_b, cache_b_idx, kb, vb, ki4, rep, u_s, yb, l, lp, n_sel_s)
        h, gates = _merge(h, ya, yb, wga, wgb, w_proj_a[l].astype(BF16), w_proj_b[l].astype(BF16),
                          w_out[l].astype(BF16), ln1_g[l], ln1_b[l], w_router[l].astype(BF16),
                          router_bias[l].astype(F32))
        h = _moe(h, gates, w_expert_gate[l].astype(BF16), w_expert_up[l].astype(BF16),
                 w_expert_down[l].astype(BF16), w_shared_gate[l].astype(BF16), w_shared_up[l].astype(BF16),
                 w_shared_down[l].astype(BF16), ln2_g[l], ln2_b[l])

    p0 = FRONT_PAD
    y_prompt = h[p0 + N_META:lp].reshape(1, seq, d)
    y_sample = h[lp:].reshape(nb, ds_, d)

    def stack(idx, width, tail):
        return (jnp.stack([r[idx][p0:lp, :width] for r in rows]).reshape((depth, 1, lp - p0) + tail),
                jnp.stack([r[idx][lp:, :width] for r in rows]).reshape((depth, nb, ds_) + tail))

    ak_p, ak_s = stack(0, 1024, (2 * A_HEADS, A_HEAD_DIM))
    av_p, av_s = stack(1, 1024, (A_HEADS, 2 * A_HEAD_DIM))
    bk_p, bk_s = stack(2, 256, (B_KV_HEADS, B_HEAD_DIM))
    bv_p, bv_s = stack(3, 256, (B_KV_HEADS, B_HEAD_DIM))
    bi_p, bi_s = stack(4, IDX_DIM, (IDX_DIM,))
    return (y_prompt, y_sample, ak_p, av_p, bk_p, bv_p, bi_p, ak_s, av_s, bk_s, bv_s, bi_s)
```

```python
import functools
import math

import numpy as np
import jax
import jax.numpy as jnp
from jax import lax
from jax.experimental import pallas as pl
from jax.experimental.pallas import tpu as pltpu

F32 = jnp.float32
BF16 = jnp.bfloat16
I32 = jnp.int32

MODEL_DEPTH = 4
CHUNK = 64
N_META = 16
ROPE_THETA = 10000.0
LN_EPS = 1e-5
NEG_INF = -1e30
DEEPNORM_ALPHA = (2 * MODEL_DEPTH) ** 0.25
A_HEADS = 8
A_HEAD_DIM = 64
B_HEADS = 8
B_KV_HEADS = 2
B_GROUP = B_HEADS // B_KV_HEADS
B_HEAD_DIM = 128
IDX_HEADS = 4
IDX_DIM = 64
TOPK_MAX = 256
N_EXPERTS = 64
TOP_K = 8
N_GROUPS = 8
GROUP_SIZE = N_EXPERTS // N_GROUPS
TOPK_GROUPS = 4
ROUTE_SCALE = 2.5

LANES = 128
D_MODEL = 1024
N_PROJ = 4992
COL_QA, COL_KA, COL_VA, COL_QB, COL_KB, COL_VB, COL_QI, COL_KIWI = (
    0, 1024, 2048, 3072, 4096, 4352, 4608, 4864)
N_RAW_PROJ = 4932

TILE_K = 512
FRONT_PAD = TILE_K - N_META
TILE_Q_A = 512
TILE_Q_B = 256
TILE_ROWS = 256
TILE_MOE = 512
ROUTE_ROWS = 64
EXPERT_BLOCK = 8
VMEM_LIMIT = 56 * 1024 * 1024
BIG_CHUNK = 1 << 30
INT_MIN = -(2 ** 31)
LOG2E = 1.4426950408889634
V_ROWS = 144
VT_PER_K = TILE_K // TILE_ROWS
COUNT_ROWS = 16
COUNT_CHAINS = 4
SOFTMAX_ROWS = 128


def _f32_key(v):
    b = int(np.float32(v).view(np.int32))
    return b ^ ((b >> 31) & 0x7FFFFFFF)


KEY_HALF_NEG = _f32_key(0.5 * NEG_INF)


def _dot(a, b):
    return jnp.dot(a, b, preferred_element_type=F32)


def _dot_nt(a, b):
    return lax.dot_general(a, b, (((1,), (1,)), ((), ())), preferred_element_type=F32)


def _layernorm(x, g, b):
    mu = jnp.mean(x, -1, keepdims=True)
    xc = x - mu
    var = jnp.mean(xc * xc, -1, keepdims=True)
    return xc * lax.rsqrt(var + LN_EPS) * g + b


def _sort_key(x):
    bits = lax.bitcast_convert_type(x, I32)
    return bits ^ ((bits >> 31) & 0x7FFFFFFF)


def _kth_largest_key(count_ge, shape, k):
    def body(b, tu):
        cand = tu | jnp.left_shift(jnp.int32(1), 31 - b)
        c = count_ge(cand ^ INT_MIN)
        return jnp.where(c >= k, cand, tu)
    tu = lax.fori_loop(0, 32, body, jnp.zeros(shape, I32))
    return tu ^ INT_MIN


def _kth_largest_key_bracketed(count_ge, top_key, k):
    def flags(b):
        return jnp.where(b, 1.0, 0.0)

    def any_set(f):
        return jnp.max(f) > 0.0

    enough = count_ge(jnp.full_like(top_key, KEY_HALF_NEG + 1)) >= k
    top_tie = enough & (count_ge(top_key) >= k)
    lo = jnp.where(top_tie, top_key, INT_MIN)
    hi = top_key
    done = top_tie | ~enough

    def gallop_cond(s):
        return (s[0] < 5) & any_set(s[3])

    def gallop(s):
        g, lo, hi, want = s
        need_lo = want > 0.5
        step = jnp.left_shift(jnp.int32(1 << 23), 2 * g)
        cand = jnp.where((g < 4) & (top_key >= INT_MIN + step), top_key - step, INT_MIN)
        ok = count_ge(cand) >= k
        lo = jnp.where(need_lo & ok, cand, lo)
        hi = jnp.where(need_lo & ~ok, cand, hi)
        return g + 1, lo, hi, flags(need_lo & ~ok)

    _, lo, hi, _ = lax.while_loop(gallop_cond, gallop, (jnp.int32(0), lo, hi, flags(~done)))

    def bisect_cond(s):
        return any_set(s[2])

    def bisect(s):
        lo, hi, open_f = s
        open_ = open_f > 0.5
        mid = lo + lax.shift_right_logical(hi - lo, jnp.full_like(lo, 1))
        c = count_ge(mid)
        ge = c >= k
        lo = jnp.where(open_ & ge, mid, lo)
        hi = jnp.where(open_ & ~ge, mid, hi)
        return lo, hi, flags(open_ & (c != k) & (hi - lo != 1))

    lo, _, _ = lax.while_loop(bisect_cond, bisect, (lo, hi, flags(~done & (hi - lo != 1))))
    return lo


def _tile_rows(j):
    return pl.ds(pl.multiple_of(j * TILE_K, TILE_K), TILE_K)


def _chunk_mask_t(j, q_chunk):
    k_idx = j * TILE_K + lax.broadcasted_iota(I32, (TILE_K, 1), 0)
    return jnp.where(k_idx >= FRONT_PAD, k_idx >> 6, BIG_CHUNK) <= q_chunk


def _flash_update_t(t_ref, m_ref, acc_ref, vt_tiles, width):
    chunks = [slice(r, r + SOFTMAX_ROWS) for r in range(0, TILE_K, SOFTMAX_ROWS)]
    for b in range(t_ref.shape[1] // width):
        cols = slice(b * width, (b + 1) * width)
        m_prev = m_ref[:, cols]
        m_new = m_prev
        for rows in chunks:
            m_new = jnp.maximum(m_new, jnp.max(t_ref[rows, cols], axis=0, keepdims=True))
        p = jnp.concatenate([jnp.exp2(t_ref[rows, cols] - m_new).astype(BF16) for rows in chunks], axis=0)
        pv = _dot(vt_tiles[0], p[0:TILE_ROWS])
        for c in range(1, VT_PER_K):
            pv = pv + _dot(vt_tiles[c], p[c * TILE_ROWS:(c + 1) * TILE_ROWS])
        acc_ref[:, cols] = jnp.exp2(m_prev - m_new) * acc_ref[:, cols] + pv
        m_ref[:, cols] = m_new


def _params(*sem):
    return pltpu.CompilerParams(dimension_semantics=sem, vmem_limit_bytes=VMEM_LIMIT)


def _ln_kernel(x_ref, g_ref, b_ref, o_ref):
    o_ref[...] = _layernorm(x_ref[...], g_ref[...], b_ref[...])


def _ln_rows(x, g, b):
    n, d = x.shape
    row = pl.BlockSpec((TILE_ROWS, d), lambda i: (i, 0))
    vec = pl.BlockSpec((1, d), lambda i: (0, 0))
    return pl.pallas_call(
        _ln_kernel, grid=(n // TILE_ROWS,), in_specs=[row, vec, vec], out_specs=row,
        out_shape=jax.ShapeDtypeStruct((n, d), F32), compiler_params=_params("parallel"),
        name="ln_in")(x, g.reshape(1, d), b.reshape(1, d))


def _proj_kernel(h_ref, w_ref, c64_ref, s64_ref, c128_ref, s128_ref,
                 qa_ref, qat_ref, kaf_ref, kab_ref, vaf_ref, vab_ref, vat_ref, qb_ref, qbt_ref,
                 kbf_ref, kbb_ref, vbf_ref, vbb_ref, vbt_ref, qi_ref, qit_ref, kiwi_ref, ki_ref,
                 ki4_ref, wt_ref):
    hb = h_ref[...].astype(BF16)
    c64, s64, c128, s128 = c64_ref[...], s64_ref[...], c128_ref[...], s128_ref[...]
    lane = lax.broadcasted_iota(I32, c64.shape, 1)
    low_half = (lane & (A_HEAD_DIM // 2)) == 0

    def rope64(y):
        swapped = jnp.where(low_half, pltpu.roll(y, LANES - 32, 1), pltpu.roll(y, 32, 1))
        return y * c64 + swapped * s64

    def rope128(y):
        return y * c128 + pltpu.roll(y, 64, 1) * s128

    def proj(col, width):
        return _dot(hb, w_ref[:, col:col + width])

    def tiles(y):
        return [y[:, t * LANES:(t + 1) * LANES] for t in range(y.shape[1] // LANES)]

    def put_values_t(vt_ref, y):
        for t, v in enumerate(tiles(y)):
            vt_ref[t, 0:LANES, :] = v.T.astype(BF16)
            vt_ref[t, LANES:V_ROWS, :] = jnp.ones((V_ROWS - LANES, v.shape[0]), BF16)

    for t, y in enumerate(tiles(proj(COL_QA, 1024))):
        r = rope64(y)
        qa_ref[:, t * LANES:(t + 1) * LANES] = r.astype(BF16)
        qat_ref[t * LANES:(t + 1) * LANES, :] = (r * (A_HEAD_DIM ** -0.5 * LOG2E)).T.astype(BF16)
    for t, y in enumerate(tiles(proj(COL_KA, 1024))):
        r = rope64(y)
        kaf_ref[:, t * LANES:(t + 1) * LANES] = r
        kab_ref[:, t * LANES:(t + 1) * LANES] = r.astype(BF16)
    y = proj(COL_VA, 1024)
    vaf_ref[...] = y
    vab_ref[...] = y.astype(BF16)
    put_values_t(vat_ref, y)
    for t, y in enumerate(tiles(proj(COL_QB, 1024))):
        r = rope128(y)
        qb_ref[:, t * LANES:(t + 1) * LANES] = r.astype(BF16)
        qbt_ref[t * LANES:(t + 1) * LANES, :] = (r * (B_HEAD_DIM ** -0.5 * LOG2E)).T.astype(BF16)
    for t, y in enumerate(tiles(proj(COL_KB, 256))):
        r = rope128(y)
        kbf_ref[:, t * LANES:(t + 1) * LANES] = r
        kbb_ref[:, t * LANES:(t + 1) * LANES] = r.astype(BF16)
    y = proj(COL_VB, 256)
    vbf_ref[...] = y
    vbb_ref[...] = y.astype(BF16)
    put_values_t(vbt_ref, y)
    for t, y in enumerate(tiles(proj(COL_QI, 256))):
        r = rope64(y)
        qi_ref[:, t * LANES:(t + 1) * LANES] = r.astype(BF16)
        qit_ref[t * LANES:(t + 1) * LANES, :] = r.T.astype(BF16)
    y = proj(COL_KIWI, LANES)
    is_key = lane < IDX_DIM
    kiwi = jnp.where(is_key, rope64(y), y * (IDX_HEADS ** -0.5))
    kiwi_ref[...] = kiwi
    ki_ref[...] = kiwi[:, 0:IDX_DIM].astype(BF16)
    ki2 = jnp.where(is_key, kiwi, pltpu.roll(kiwi, IDX_DIM, 1)).astype(BF16)
    ki4_ref[:, 0:LANES] = ki2
    ki4_ref[:, LANES:2 * LANES] = ki2
    wt_ref[...] = kiwi.T[IDX_DIM:IDX_DIM + 8, :]


def _project(h, w1, tabs):
    n = h.shape[0]

    def row(width):
        return pl.BlockSpec((TILE_ROWS, width), lambda i: (i, 0))

    def col(height):
        return (pl.BlockSpec((height, TILE_ROWS), lambda i: (0, i)), (height, n))

    def vals(heads):
        return (pl.BlockSpec((heads, None, V_ROWS, TILE_ROWS), lambda i: (0, i, 0, 0)),
                (heads, n // TILE_ROWS, V_ROWS, TILE_ROWS))

    def rows(width):
        return (row(width), (n, width))

    outs = [rows(1024) + (BF16,), col(1024) + (BF16,), rows(1024) + (F32,), rows(1024) + (BF16,),
            rows(1024) + (F32,), rows(1024) + (BF16,), vals(A_HEADS) + (BF16,),
            rows(1024) + (BF16,), col(1024) + (BF16,), rows(256) + (F32,), rows(256) + (BF16,),
            rows(256) + (F32,), rows(256) + (BF16,), vals(B_KV_HEADS) + (BF16,),
            rows(256) + (BF16,), col(256) + (BF16,), rows(LANES) + (F32,), rows(IDX_DIM) + (BF16,),
            rows(256) + (BF16,), col(8) + (F32,)]
    return pl.pallas_call(
        _proj_kernel, grid=(n // TILE_ROWS,),
        in_specs=[row(D_MODEL), pl.BlockSpec((D_MODEL, N_PROJ), lambda i: (0, 0))] + [row(LANES)] * 4,
        out_specs=[spec for spec, _, _ in outs],
        out_shape=[jax.ShapeDtypeStruct(shape, dt) for _, shape, dt in outs],
        compiler_params=_params("parallel"), name="in_proj")(h, w1, *tabs)


def _subln(o1, o2, lam, g, lam_init):
    o = o1 - lam * o2
    y = o * lax.rsqrt(jnp.mean(o * o, -1, keepdims=True) + LN_EPS) * g
    return y * (1.0 - lam_init)


def _diff_p_kernel(lam_ref, qt_ref, k_ref, vt_ref, g_ref, o_ref, qq_sc, ta_sc, tb_sc, m_sc, acc_sc, *,
                   lam_init):
    tq = qt_ref.shape[1]
    i = pl.program_id(1)
    qt = qt_ref[...]
    row = lax.broadcasted_iota(I32, (LANES, 1), 0)
    zero = jnp.zeros_like(qt)
    qq_sc[...] = jnp.concatenate([jnp.where(row < A_HEAD_DIM, qt, zero),
                                  jnp.where(row >= A_HEAD_DIM, qt, zero)], axis=1)
    q_pos = i * tq + lax.broadcasted_iota(I32, (1, tq), 1)
    q_chunk = jnp.concatenate([q_pos, q_pos], axis=1) >> 6
    m_sc[...] = jnp.full(m_sc.shape, NEG_INF, F32)
    acc_sc[...] = jnp.zeros(acc_sc.shape, F32)
    n_kv = ((i + 1) * tq + TILE_K - 1) // TILE_K

    def scores(j):
        return _dot(k_ref[_tile_rows(j), :], qq_sc[...])

    def update(t_ref, j):
        _flash_update_t(t_ref, m_sc, acc_sc, [vt_ref[VT_PER_K * j + c] for c in range(VT_PER_K)], tq)

    def masked_scores(j):
        return jnp.where(_chunk_mask_t(j, q_chunk), scores(j), NEG_INF)

    last = n_kv - 1
    ta_sc[...] = masked_scores(0)

    @pl.when(n_kv == 1)
    def _():
        update(ta_sc, 0)

    def pair(j):
        tb_sc[...] = scores(j + 1)
        update(ta_sc, j)
        ta_sc[...] = scores(j + 2)
        update(tb_sc, j + 1)

    n_quads = jnp.maximum(n_kv - 2, 0) // 4

    def quad(r, carry):
        pair(4 * r)
        pair(4 * r + 2)
        return carry

    lax.fori_loop(0, n_quads, quad, 0)
    extra_pair = (n_kv > 1) & (last - 4 * n_quads >= 3)

    @pl.when(extra_pair)
    def _():
        pair(4 * n_quads)

    j_cur = 4 * n_quads + 2 * extra_pair.astype(I32)

    @pl.when((n_kv > 1) & (last - j_cur == 1))
    def _():
        tb_sc[...] = masked_scores(last)
        update(ta_sc, j_cur)
        update(tb_sc, last)

    @pl.when((n_kv > 1) & (last - j_cur == 2))
    def _():
        tb_sc[...] = scores(j_cur + 1)
        update(ta_sc, j_cur)
        ta_sc[...] = masked_scores(last)
        update(tb_sc, j_cur + 1)
        update(ta_sc, last)

    acc = acc_sc[...]
    outs = [acc[0:LANES, mp * tq:(mp + 1) * tq] / acc[LANES:LANES + 1, mp * tq:(mp + 1) * tq] for mp in range(2)]
    o = outs[0] - lam_ref[0] * outs[1]
    y = o * lax.rsqrt(jnp.mean(o * o, axis=0, keepdims=True) + LN_EPS) * g_ref[...] * (1.0 - lam_init)
    o_ref[...] = y.T.astype(BF16)


def _diff_attn_prompt(lam, qat, ka, vat, g_col, lp, lam_init):
    n = ka.shape[0]
    return pl.pallas_call(
        functools.partial(_diff_p_kernel, lam_init=lam_init),
        grid=(A_HEADS, lp // TILE_Q_A),
        in_specs=[pl.BlockSpec(memory_space=pltpu.SMEM),
                  pl.BlockSpec((LANES, TILE_Q_A), lambda h, i: (h, i)),
                  pl.BlockSpec((lp, LANES), lambda h, i: (0, h)),
                  pl.BlockSpec((None, lp // TILE_ROWS, V_ROWS, TILE_ROWS), lambda h, i: (h, 0, 0, 0)),
                  pl.BlockSpec((LANES, 1), lambda h, i: (0, 0))],
        out_specs=pl.BlockSpec((TILE_Q_A, LANES), lambda h, i: (i, h)),
        out_shape=jax.ShapeDtypeStruct((n, A_HEADS * LANES), BF16),
        scratch_shapes=[pltpu.VMEM((LANES, 2 * TILE_Q_A), BF16),
                        pltpu.VMEM((TILE_K, 2 * TILE_Q_A), F32), pltpu.VMEM((TILE_K, 2 * TILE_Q_A), F32),
                        pltpu.VMEM((1, 2 * TILE_Q_A), F32), pltpu.VMEM((V_ROWS, 2 * TILE_Q_A), F32)],
        compiler_params=_params("parallel", "parallel"), name="diff_attn_prompt")(lam, qat, ka, vat, g_col)


def _diff_s_kernel(lam_ref, q_ref, ck_ref, cv_ref, kn_ref, vn_ref, g_ref, prev_ref, o_ref, *, lam_init, n_keys):
    del prev_ref
    ds_ = q_ref.shape[0]
    past = ck_ref.shape[0]
    pad = n_keys - past - ds_
    lane = lax.broadcasted_iota(I32, (1, LANES), 1)
    valid = lax.broadcasted_iota(I32, (1, n_keys), 1) < past + ds_
    for h in range(A_HEADS):
        cols = slice(h * LANES, (h + 1) * LANES)
        q = q_ref[:, cols]
        zero = jnp.zeros_like(q)
        k = jnp.concatenate([ck_ref[:, cols].astype(BF16), kn_ref[:, cols], jnp.zeros((pad, LANES), BF16)], 0)
        v = jnp.concatenate([cv_ref[:, cols].astype(BF16), vn_ref[:, cols], jnp.zeros((pad, LANES), BF16)], 0)
        qz = jnp.concatenate([jnp.where(lane < A_HEAD_DIM, q, zero), jnp.where(lane >= A_HEAD_DIM, q, zero)], 0)
        s = jnp.where(valid, _dot_nt(qz, k) * (A_HEAD_DIM ** -0.5), NEG_INF)
        p = jnp.exp(s - jnp.max(s, axis=1, keepdims=True))
        o = _dot(p.astype(BF16), v) / jnp.sum(p, axis=1, keepdims=True)
        o_ref[:, cols] = _subln(o[0:ds_], o[ds_:2 * ds_], lam_ref[0], g_ref[...], lam_init).astype(BF16)


def _diff_attn_sample(lam, qa, cache_k, cache_v, ka, va, g, ya, layer, lp, lam_init):
    _, nb, past, width = cache_k.shape
    n = qa.shape[0]
    ds_ = (n - lp) // nb
    n_keys = -(-(past + ds_) // LANES) * LANES
    base = lp // ds_
    new = pl.BlockSpec((ds_, width), lambda b: (base + b, 0))
    cache = pl.BlockSpec((None, None, past, width), lambda b: (layer, b, 0, 0))
    return pl.pallas_call(
        functools.partial(_diff_s_kernel, lam_init=lam_init, n_keys=n_keys), grid=(nb,),
        in_specs=[pl.BlockSpec(memory_space=pltpu.SMEM), new, cache, cache, new, new,
                  pl.BlockSpec((1, LANES), lambda b: (0, 0)), pl.BlockSpec(memory_space=pl.ANY)],
        out_specs=new, out_shape=jax.ShapeDtypeStruct(ya.shape, ya.dtype),
        input_output_aliases={7: 0},
        compiler_params=_params("parallel"), name="diff_attn_sample")(lam, qa, cache_k, cache_v, ka, va, g, ya)


def _index_scores(qiz, w_cols, ki4, rows):
    r = _dot_nt(qiz, ki4)
    sc = w_cols[0] * jnp.maximum(r[0:rows], 0.0)
    for h in range(1, IDX_HEADS):
        sc = sc + w_cols[h] * jnp.maximum(r[h * rows:(h + 1) * rows], 0.0)
    return sc


def _stack_idx_queries(qi, rows):
    lane = lax.broadcasted_iota(I32, (1, IDX_HEADS * IDX_DIM), 1)
    zero = jnp.zeros_like(qi)
    return [jnp.where((lane >= IDX_DIM * h) & (lane < IDX_DIM * (h + 1)), qi, zero) for h in range(IDX_HEADS)]


def _selected(key, t, need, seen, u):
    eq = key == t
    prefix = _dot(jnp.where(eq, 1.0, 0.0).astype(BF16), u)
    sel = ((key > t) | (eq & (seen + prefix <= need))) & (key > KEY_HALF_NEG)
    return sel, seen + prefix[:, -1:]


def _dsa_p_kernel(qbt_ref, qit_ref, wt_ref, kb_ref, vbt_ref, ki_ref, tri_ref, o_ref,
                  key_sc, qq_sc, ta_sc, tb_sc, sel_sc, m_sc, acc_sc, *, n_sel):
    tq = qbt_ref.shape[1]
    i = pl.program_id(0)
    n_kv = ((i + 1) * tq + TILE_K - 1) // TILE_K
    w = wt_ref[...] * (IDX_DIM ** -0.5)
    w_rows = [w[h:h + 1] for h in range(IDX_HEADS)]
    q_chunk = (i * tq + lax.broadcasted_iota(I32, (1, tq), 1)) >> 6

    last = n_kv - 1

    def put_products(t_ref, j):
        ki = ki_ref[_tile_rows(j), :]
        for h in range(IDX_HEADS):
            t_ref[:, h * tq:(h + 1) * tq] = _dot(ki, qit_ref[h * IDX_DIM:(h + 1) * IDX_DIM, :])

    def put_keys(t_ref, j, top):
        sc = None
        for h in range(IDX_HEADS):
            r = w_rows[h] * jnp.maximum(t_ref[:, h * tq:(h + 1) * tq], 0.0)
            sc = r if sc is None else sc + r
        key = _sort_key(jnp.where(_chunk_mask_t(j, q_chunk), sc, NEG_INF))
        key_sc[j] = key
        return jnp.maximum(top, jnp.max(key, axis=0, keepdims=True))

    put_products(ta_sc, 0)

    def score_pair(r, top):
        j = 2 * r
        j1 = jnp.minimum(j + 1, last)
        put_products(tb_sc, j1)
        top = put_keys(ta_sc, j, top)
        put_products(ta_sc, jnp.minimum(j + 2, last))
        return put_keys(tb_sc, j1, top)

    top_key = lax.fori_loop(0, (n_kv + 1) // 2, score_pair, jnp.full((1, tq), INT_MIN, I32))

    def count_ge(c):
        def body(j, acc):
            parts = [None] * COUNT_CHAINS
            for n, r in enumerate(range(0, TILE_K, COUNT_ROWS)):
                hit = jnp.where(key_sc[j, r:r + COUNT_ROWS, :] >= c, 1.0, 0.0)
                parts[n % COUNT_CHAINS] = hit if parts[n % COUNT_CHAINS] is None else parts[n % COUNT_CHAINS] + hit
            return acc + functools.reduce(lambda a, b: a + b, parts)
        return jnp.sum(lax.fori_loop(0, n_kv, body, jnp.zeros((COUNT_ROWS, tq), F32)), axis=0, keepdims=True)

    thr = _kth_largest_key_bracketed(count_ge, top_key, float(n_sel))
    need = float(n_sel) - count_ge(thr + 1)
    need = jnp.where(thr > KEY_HALF_NEG, need, 0.0)
    floor_key = jnp.maximum(thr, KEY_HALF_NEG)

    for c in range(B_KV_HEADS):
        for g in range(B_GROUP):
            hh = c * B_GROUP + g
            qq_sc[c, :, g * tq:(g + 1) * tq] = qbt_ref[hh * LANES:(hh + 1) * LANES, :]
    m_sc[...] = jnp.full(m_sc.shape, NEG_INF, F32)
    acc_sc[...] = jnp.zeros(acc_sc.shape, F32)

    def put_scores(t_ref, j, c):
        s = _dot(kb_ref[_tile_rows(j), c * LANES:(c + 1) * LANES], qq_sc[c])
        sel = sel_sc[...] > 0.5
        for g in range(B_GROUP):
            t_ref[:, g * tq:(g + 1) * tq] = jnp.where(sel, s[:, g * tq:(g + 1) * tq], NEG_INF)

    def update(t_ref, j, c):
        _flash_update_t(t_ref, m_sc.at[c], acc_sc.at[c],
                        [vbt_ref[c, VT_PER_K * j + cc] for cc in range(VT_PER_K)], tq)

    def select(j, seen):
        key = key_sc[j]
        tie = key == thr
        half = TILE_K // 2
        ones = jnp.where(tie, 1.0, 0.0).astype(BF16)
        for r in range(2):
            rows = slice(r * half, (r + 1) * half)
            prefix = seen + _dot(tri_ref[...], ones[rows])
            sel_sc[rows, :] = jnp.where((key[rows] > floor_key) | (tie[rows] & (prefix <= need)), 1.0, 0.0)
            seen = prefix[half - 1:half, :]
        return seen

    seen0 = select(0, jnp.zeros((1, tq), F32))
    put_scores(ta_sc, 0, 0)

    def attend(j, seen):
        nxt = jnp.minimum(j + 1, last)
        put_scores(tb_sc, j, 1)
        update(ta_sc, j, 0)
        seen = select(nxt, seen)
        put_scores(ta_sc, nxt, 0)
        update(tb_sc, j, 1)
        return seen

    def attend_four(r, seen):
        for u in range(4):
            seen = attend(4 * r + u, seen)
        return seen

    seen_mid = lax.fori_loop(0, n_kv // 4, attend_four, seen0)
    lax.fori_loop(4 * (n_kv // 4), n_kv, attend, seen_mid)
    for c in range(B_KV_HEADS):
        acc = acc_sc[c]
        for g in range(B_GROUP):
            hh = c * B_GROUP + g
            a = acc[:, g * tq:(g + 1) * tq]
            o_ref[:, hh * LANES:(hh + 1) * LANES] = (a[0:LANES] / a[LANES:LANES + 1]).T.astype(BF16)


def _dsa_prompt(qbt, qit, wt, kb, vbt, ki, tri, lp, n_sel):
    n = kb.shape[0]
    tq = TILE_Q_B

    def col(height):
        return pl.BlockSpec((height, tq), lambda i: (0, i))

    def resident(shape):
        return pl.BlockSpec(shape, lambda i: (0,) * len(shape), pipeline_mode=pl.Buffered(1))

    return pl.pallas_call(
        functools.partial(_dsa_p_kernel, n_sel=n_sel), grid=(lp // tq,),
        in_specs=[col(1024), col(256), col(8), resident((lp, 256)),
                  resident((B_KV_HEADS, lp // TILE_ROWS, V_ROWS, TILE_ROWS)), resident((lp, IDX_DIM)),
                  resident((TILE_K // 2, TILE_K // 2))],
        out_specs=pl.BlockSpec((tq, 1024), lambda i: (i, 0)),
        out_shape=jax.ShapeDtypeStruct((n, 1024), BF16),
        scratch_shapes=[pltpu.VMEM((lp // TILE_K, TILE_K, tq), I32),
                        pltpu.VMEM((B_KV_HEADS, LANES, B_GROUP * tq), BF16),
                        pltpu.VMEM((TILE_K, B_GROUP * tq), F32), pltpu.VMEM((TILE_K, B_GROUP * tq), F32),
                        pltpu.VMEM((TILE_K, tq), F32),
                        pltpu.VMEM((B_KV_HEADS, 1, B_GROUP * tq), F32),
                        pltpu.VMEM((B_KV_HEADS, V_ROWS, B_GROUP * tq), F32)],
        compiler_params=_params("parallel"), name="dsa_prompt")(qbt, qit, wt, kb, vbt, ki, tri)


def _dsa_s_kernel(qb_ref, qi_ref, kiwi_ref, ckb_ref, cvb_ref, cki_ref, kbn_ref, vbn_ref, ki4n_ref,
                  rep_ref, u_ref, prev_ref, o_ref, *, n_sel, n_keys):
    del prev_ref
    tq = qb_ref.shape[0]
    past = ckb_ref.shape[0]
    pad = n_keys - past - tq
    width = B_KV_HEADS * B_HEAD_DIM
    kb = jnp.concatenate([ckb_ref[...].astype(BF16), kbn_ref[...], jnp.zeros((pad, width), BF16)], 0)
    vb = jnp.concatenate([cvb_ref[...].astype(BF16), vbn_ref[...], jnp.zeros((pad, width), BF16)], 0)
    ki4_past = _dot(cki_ref[...].astype(BF16), rep_ref[...]).astype(BF16)
    ki4 = jnp.concatenate([ki4_past, ki4n_ref[...], jnp.zeros((pad, width), BF16)], 0)
    qiz = jnp.concatenate(_stack_idx_queries(qi_ref[...], tq), 0)
    wq = kiwi_ref[...] * (IDX_DIM ** -0.5)
    w_cols = [wq[:, IDX_DIM + h:IDX_DIM + h + 1] for h in range(IDX_HEADS)]
    sc = _index_scores(qiz, w_cols, ki4, tq)
    valid = lax.broadcasted_iota(I32, (1, n_keys), 1) < past + tq
    key = _sort_key(jnp.where(valid, sc, NEG_INF))

    def count_ge(c):
        return jnp.sum(jnp.where(key >= c, 1.0, 0.0), axis=1, keepdims=True)

    t = _kth_largest_key(count_ge, (tq, 1), float(n_sel))
    need = float(n_sel) - count_ge(t + 1)
    sel, _ = _selected(key, t, need, jnp.zeros((tq, 1), F32), u_ref[...])
    for c in range(B_KV_HEADS):
        cols = slice(c * LANES, (c + 1) * LANES)
        q = jnp.concatenate([qb_ref[:, (c * B_GROUP + g) * LANES:(c * B_GROUP + g + 1) * LANES]
                             for g in range(B_GROUP)], 0)
        s_all = _dot_nt(q, kb[:, cols]) * (B_HEAD_DIM ** -0.5)
        for g in range(B_GROUP):
            hh = c * B_GROUP + g
            s = jnp.where(sel, s_all[g * tq:(g + 1) * tq], NEG_INF)
            p = jnp.exp(s - jnp.max(s, axis=1, keepdims=True))
            o = _dot(p.astype(BF16), vb[:, cols]) / jnp.sum(p, axis=1, keepdims=True)
            o_ref[:, hh * LANES:(hh + 1) * LANES] = o.astype(BF16)


def _dsa_sample(qb, qi, kiwi, cache_k, cache_v, cache_idx, kb, vb, ki4, rep, u, yb, layer, lp, n_sel):
    _, nb, past, _ = cache_k.shape
    n = qb.shape[0]
    ds_ = (n - lp) // nb
    n_keys = u.shape[0]
    base = lp // ds_

    def new(width):
        return pl.BlockSpec((ds_, width), lambda b: (base + b, 0))

    def cache(width):
        return pl.BlockSpec((None, None, past, width), lambda b: (layer, b, 0, 0))

    def const(shape):
        return pl.BlockSpec(shape, lambda b: (0, 0))

    return pl.pallas_call(
        functools.partial(_dsa_s_kernel, n_sel=n_sel, n_keys=n_keys), grid=(nb,),
        in_specs=[new(1024), new(256), new(LANES), cache(256), cache(256), cache(IDX_DIM),
                  new(256), new(256), new(256), const(rep.shape), const(u.shape),
                  pl.BlockSpec(memory_space=pl.ANY)],
        out_specs=new(1024), out_shape=jax.ShapeDtypeStruct(yb.shape, yb.dtype),
        input_output_aliases={11: 0},
        compiler_params=_params("parallel"), name="dsa_sample")(
            qb, qi, kiwi, cache_k, cache_v, cache_idx, kb, vb, ki4, rep, u, yb)


def _merge_kernel(h_ref, ya_ref, yb_ref, wga_ref, wgb_ref, wpa_ref, wpb_ref, wo_ref, g_ref, b_ref,
                  wr_ref, bias_ref, o_ref, gates_ref):
    h = h_ref[...]
    hb = h.astype(BF16)
    mixed = (jax.nn.sigmoid(_dot(hb, wga_ref[...])) * _dot(ya_ref[...], wpa_ref[...])
             + jax.nn.sigmoid(_dot(hb, wgb_ref[...])) * _dot(yb_ref[...], wpb_ref[...]))
    y = _dot(mixed.astype(BF16), wo_ref[...])
    h1 = _layernorm(DEEPNORM_ALPHA * h + y, g_ref[...], b_ref[...])
    o_ref[...] = h1
    gates_ref[...] = _route_t(jax.nn.sigmoid(_dot_nt(wr_ref[...], h1.astype(BF16))), bias_ref[...])


def _merge(h, ya, yb, wga, wgb, wpa, wpb, wo, g, b, wr_t, bias):
    n, d = h.shape
    row = pl.BlockSpec((TILE_ROWS, d), lambda i: (i, 0))
    mat = pl.BlockSpec((d, d), lambda i: (0, 0))
    vec = pl.BlockSpec((1, d), lambda i: (0, 0))
    return pl.pallas_call(
        _merge_kernel, grid=(n // TILE_ROWS,),
        in_specs=[row, row, row] + [mat] * 5 + [vec, vec, pl.BlockSpec((N_EXPERTS, d), lambda i: (0, 0)),
                                                 pl.BlockSpec((N_EXPERTS, 1), lambda i: (0, 0))],
        out_specs=[row, pl.BlockSpec((N_EXPERTS, TILE_ROWS), lambda i: (0, i))],
        out_shape=[jax.ShapeDtypeStruct((n, d), F32), jax.ShapeDtypeStruct((N_EXPERTS, n), F32)],
        compiler_params=_params("parallel"), name="merge_ln1_route")(
            h, ya, yb, wga, wgb, wpa, wpb, wo, g.reshape(1, d), b.reshape(1, d), wr_t,
            bias.reshape(N_EXPERTS, 1))


def _route_t(s, bias):
    sel = s + bias
    row_f = lax.broadcasted_iota(I32, (N_EXPERTS, 1), 0).astype(F32)
    neg = -jnp.inf

    def first_argmax(x, m, rows):
        return jnp.min(jnp.where(x == m, rows, float(N_EXPERTS)), axis=0, keepdims=True)

    groups = [slice(g * GROUP_SIZE, (g + 1) * GROUP_SIZE) for g in range(N_GROUPS)]
    g_score = []
    for rows in groups:
        x = sel[rows]
        m1 = jnp.max(x, axis=0, keepdims=True)
        x = jnp.where(row_f[rows] == first_argmax(x, m1, row_f[rows]), neg, x)
        g_score.append(m1 + jnp.max(x, axis=0, keepdims=True))
    kept = []
    for g, rows in enumerate(groups):
        ahead = jnp.zeros_like(g_score[g])
        for o in range(N_GROUPS):
            if o != g:
                beats = (g_score[o] > g_score[g]) | ((g_score[o] == g_score[g]) & (o < g))
                ahead = ahead + jnp.where(beats, 1.0, 0.0)
        kept.append(jnp.where(ahead < float(TOPK_GROUPS), sel[rows], NEG_INF))
    x = jnp.concatenate(kept, axis=0)
    chosen = None
    for _ in range(TOP_K):
        hit = row_f == first_argmax(x, jnp.max(x, axis=0, keepdims=True), row_f)
        chosen = hit if chosen is None else (chosen | hit)
        x = jnp.where(hit, neg, x)
    w = jnp.where(chosen, s, 0.0)
    return w / jnp.sum(w, axis=0, keepdims=True) * ROUTE_SCALE


def _silu(x):
    return x * jax.nn.sigmoid(x)


def _moe_kernel(h_ref, gates_ref, wg_ref, wu_ref, wd_ref, sg_ref, su_ref, sd_ref, g_ref, b_ref, o_ref,
                xb_sc, acc_sc):
    step = pl.program_id(1)

    @pl.when(step == 0)
    def _():
        xb = h_ref[...].astype(BF16)
        xb_sc[...] = xb
        a = _silu(_dot(xb, sg_ref[...])) * _dot(xb, su_ref[...])
        acc_sc[...] = _dot(a.astype(BF16), sd_ref[...])

    xb = xb_sc[...]
    g_rows = gates_ref[...]
    gates = jnp.concatenate([g_rows, jnp.zeros((LANES - EXPERT_BLOCK, g_rows.shape[1]), F32)], axis=0).T
    acts = []
    for e in range(EXPERT_BLOCK):
        a = _silu(_dot(xb, wg_ref[e])) * _dot(xb, wu_ref[e]) * gates[:, e:e + 1]
        acts.append(a.astype(BF16))
    wd = wd_ref[...]
    acc_sc[...] += _dot(jnp.concatenate(acts, axis=1), wd.reshape(wd.shape[0] * wd.shape[1], wd.shape[2]))

    @pl.when(step == pl.num_programs(1) - 1)
    def _():
        o_ref[...] = _layernorm(DEEPNORM_ALPHA * h_ref[...] + acc_sc[...], g_ref[...], b_ref[...])


def _moe(h, gates, wg, wu, wd, sg, su, sd, g, b):
    n, d = h.shape
    f = wg.shape[-1]
    row = pl.BlockSpec((TILE_MOE, d), lambda r, e: (r, 0))
    vec = pl.BlockSpec((1, d), lambda r, e: (0, 0))
    return pl.pallas_call(
        _moe_kernel, grid=(n // TILE_MOE, N_EXPERTS // EXPERT_BLOCK),
        in_specs=[row, pl.BlockSpec((EXPERT_BLOCK, TILE_MOE), lambda r, e: (e, r)),
                  pl.BlockSpec((EXPERT_BLOCK, d, f), lambda r, e: (e, 0, 0)),
                  pl.BlockSpec((EXPERT_BLOCK, d, f), lambda r, e: (e, 0, 0)),
                  pl.BlockSpec((EXPERT_BLOCK, f, d), lambda r, e: (e, 0, 0)),
                  pl.BlockSpec((d, sg.shape[1]), lambda r, e: (0, 0)),
                  pl.BlockSpec((d, sg.shape[1]), lambda r, e: (0, 0)),
                  pl.BlockSpec((sg.shape[1], d), lambda r, e: (0, 0)), vec, vec],
        out_specs=row, out_shape=jax.ShapeDtypeStruct((n, d), F32),
        scratch_shapes=[pltpu.VMEM((TILE_MOE, d), BF16), pltpu.VMEM((TILE_MOE, d), F32)],
        compiler_params=_params("parallel", "arbitrary"), name="moe_ln2")(
            h, gates, wg, wu, wd, sg, su, sd, g.reshape(1, d), b.reshape(1, d))


def _rope_tables(pos):
    def table(half, reps):
        inv = ROPE_THETA ** (-jnp.arange(half, dtype=F32) / half)
        ang = pos.astype(F32)[:, None] * inv[None, :]
        cos, sin = jnp.cos(ang), jnp.sin(ang)
        return (jnp.tile(jnp.concatenate([cos, cos], -1), (1, reps)),
                jnp.tile(jnp.concatenate([-sin, sin], -1), (1, reps)))
    c64, s64 = table(A_HEAD_DIM // 2, 2)
    c128, s128 = table(B_HEAD_DIM // 2, 1)
    return c64, s64, c128, s128


def kernel(x_prompt, x_sample, cache_a_k, cache_a_v, cache_b_k, cache_b_v, cache_b_idx, meta_tokens, ln_in_g, ln_in_b, w_in, lam_q1, lam_k1, lam_q2, lam_k2, subln_g, w_proj_a, w_proj_b, w_out, ln1_g, ln1_b, w_router, router_bias, w_expert_gate, w_expert_up, w_expert_down, w_shared_gate, w_shared_up, w_shared_down, ln2_g, ln2_b):
    batch, seq, d = x_prompt.shape
    nb, ds_, _ = x_sample.shape
    depth, _, past = cache_a_k.shape[:3]
    assert batch == 1 and d == D_MODEL and seq % TILE_K == 0 and ds_ == CHUNK
    lp = FRONT_PAD + N_META + seq
    ns = nb * ds_
    n = lp + ns
    assert n % TILE_MOE == 0 and past % 16 == 0
    n_sel_p = min(TOPK_MAX, seq // 4)
    n_sel_s = min(TOPK_MAX, (past + ds_) // 4)
    n_keys_s = -(-(past + ds_) // LANES) * LANES

    x_all = jnp.concatenate([jnp.zeros((FRONT_PAD, d), F32), meta_tokens.astype(F32), x_prompt[0],
                             x_sample.reshape(ns, d)], 0)
    pos = jnp.concatenate([jnp.maximum(jnp.arange(lp, dtype=jnp.int32) - FRONT_PAD, 0),
                           jnp.tile(past + jnp.arange(ds_, dtype=jnp.int32), nb)])
    tabs = _rope_tables(pos)
    tri_p = jnp.tril(jnp.ones((TILE_K // 2, TILE_K // 2), BF16))
    u_s = jnp.triu(jnp.ones((n_keys_s, n_keys_s), BF16))
    rep = jnp.tile(jnp.eye(IDX_DIM, dtype=BF16), (1, IDX_HEADS))
    ck_a = cache_a_k.reshape(depth, nb, past, -1)
    cv_a = cache_a_v.reshape(depth, nb, past, -1)
    ck_b = cache_b_k.reshape(depth, nb, past, -1)
    cv_b = cache_b_v.reshape(depth, nb, past, -1)

    h = _ln_rows(x_all, ln_in_g, ln_in_b)
    rows = []
    for l in range(depth):
        lam_init = 0.8 - 0.6 * math.exp(-0.3 * l)
        lam = (jnp.exp(jnp.sum(lam_q1[l].astype(F32) * lam_k1[l].astype(F32)))
               - jnp.exp(jnp.sum(lam_q2[l].astype(F32) * lam_k2[l].astype(F32))) + lam_init).reshape(1)
        w1 = jnp.concatenate([w_in[l][:, :N_RAW_PROJ], jnp.zeros((d, N_PROJ - N_RAW_PROJ), F32)], 1).astype(BF16)
        wga = w_in[l][:, N_RAW_PROJ:N_RAW_PROJ + d].astype(BF16)
        wgb = w_in[l][:, N_RAW_PROJ + d:].astype(BF16)
        (qa, qat, ka_f, ka, va_f, va, vat, qb, qbt, kb_f, kb, vb_f, vb, vbt, qi, qit, kiwi, ki, ki4,
         wt) = _project(h, w1, tabs)
        rows.append((ka_f, va_f, kb_f, vb_f, kiwi))
        g = subln_g[l].astype(F32)
        ya = _diff_attn_prompt(lam, qat, ka, vat, g.reshape(LANES, 1), lp, lam_init)
        ya = _diff_attn_sample(lam, qa, ck_a, cv_a, ka, va, g.reshape(1, LANES), ya, l, lp, lam_init)
        yb = _dsa_prompt(qbt, qit, wt, kb, vbt, ki, tri_p, lp, n_sel_p)
        yb = _dsa_sample(qb, qi, kiwi, ck_b, cv_b, cache_b_idx, kb, vb, ki4, rep, u_s, yb, l, lp, n_sel_s)
        h, gates = _merge(h, ya, yb, wga, wgb, w_proj_a[l].astype(BF16), w_proj_b[l].astype(BF16),
                          w_out[l].astype(BF16), ln1_g[l], ln1_b[l], w_router[l].T.astype(BF16),
                          router_bias[l].astype(F32))
        h = _moe(h, gates, w_expert_gate[l].astype(BF16), w_expert_up[l].astype(BF16),
                 w_expert_down[l].astype(BF16), w_shared_gate[l].astype(BF16), w_shared_up[l].astype(BF16),
                 w_shared_down[l].astype(BF16), ln2_g[l], ln2_b[l])

    p0 = FRONT_PAD
    y_prompt = h[p0 + N_META:lp].reshape(1, seq, d)
    y_sample = h[lp:].reshape(nb, ds_, d)

    def stack(idx, width, tail):
        return (jnp.stack([r[idx][p0:lp, :width] for r in rows]).reshape((depth, 1, lp - p0) + tail),
                jnp.stack([r[idx][lp:, :width] for r in rows]).reshape((depth, nb, ds_) + tail))

    ak_p, ak_s = stack(0, 1024, (2 * A_HEADS, A_HEAD_DIM))
    av_p, av_s = stack(1, 1024, (A_HEADS, 2 * A_HEAD_DIM))
    bk_p, bk_s = stack(2, 256, (B_KV_HEADS, B_HEAD_DIM))
    bv_p, bv_s = stack(3, 256, (B_KV_HEADS, B_HEAD_DIM))
    bi_p, bi_s = stack(4, IDX_DIM, (IDX_DIM,))
    return (y_prompt, y_sample, ak_p, av_p, bk_p, bv_p, bi_p, ak_s, av_s, bk_s, bv_s, bi_s)
```

```python
import functools
import math

import numpy as np
import jax
import jax.numpy as jnp
from jax import lax
from jax.experimental import pallas as pl
from jax.experimental.pallas import tpu as pltpu

F32 = jnp.float32
BF16 = jnp.bfloat16
I32 = jnp.int32

MODEL_DEPTH = 4
CHUNK = 64
N_META = 16
ROPE_THETA = 10000.0
LN_EPS = 1e-5
NEG_INF = -1e30
DEEPNORM_ALPHA = (2 * MODEL_DEPTH) ** 0.25
A_HEADS = 8
A_HEAD_DIM = 64
B_HEADS = 8
B_KV_HEADS = 2
B_GROUP = B_HEADS // B_KV_HEADS
B_HEAD_DIM = 128
IDX_HEADS = 4
IDX_DIM = 64
TOPK_MAX = 256
N_EXPERTS = 64
TOP_K = 8
N_GROUPS = 8
GROUP_SIZE = N_EXPERTS // N_GROUPS
TOPK_GROUPS = 4
ROUTE_SCALE = 2.5

LANES = 128
D_MODEL = 1024
N_PROJ = 4992
COL_QA, COL_KA, COL_VA, COL_QB, COL_KB, COL_VB, COL_QI, COL_KIWI = (
    0, 1024, 2048, 3072, 4096, 4352, 4608, 4864)
N_RAW_PROJ = 4932

TILE_K = 512
FRONT_PAD = TILE_K - N_META
TILE_Q_A = 512
TILE_Q_B = 256
TILE_ROWS = 256
TILE_MOE = 512
ROUTE_ROWS = 64
EXPERT_BLOCK = 8
VMEM_LIMIT = 56 * 1024 * 1024
BIG_CHUNK = 1 << 30
INT_MIN = -(2 ** 31)
LOG2E = 1.4426950408889634
V_ROWS = 144
VT_PER_K = TILE_K // TILE_ROWS
COUNT_ROWS = 16
COUNT_CHAINS = 4
SOFTMAX_ROWS = 256


def _f32_key(v):
    b = int(np.float32(v).view(np.int32))
    return b ^ ((b >> 31) & 0x7FFFFFFF)


KEY_HALF_NEG = _f32_key(0.5 * NEG_INF)


def _dot(a, b):
    return jnp.dot(a, b, preferred_element_type=F32)


def _dot_nt(a, b):
    return lax.dot_general(a, b, (((1,), (1,)), ((), ())), preferred_element_type=F32)


def _layernorm(x, g, b):
    mu = jnp.mean(x, -1, keepdims=True)
    xc = x - mu
    var = jnp.mean(xc * xc, -1, keepdims=True)
    return xc * lax.rsqrt(var + LN_EPS) * g + b


def _sort_key(x):
    bits = lax.bitcast_convert_type(x, I32)
    return bits ^ ((bits >> 31) & 0x7FFFFFFF)


def _kth_largest_key(count_ge, shape, k):
    def body(b, tu):
        cand = tu | jnp.left_shift(jnp.int32(1), 31 - b)
        c = count_ge(cand ^ INT_MIN)
        return jnp.where(c >= k, cand, tu)
    tu = lax.fori_loop(0, 32, body, jnp.zeros(shape, I32))
    return tu ^ INT_MIN


def _kth_largest_key_bracketed(count_ge, top_key, k):
    def flags(b):
        return jnp.where(b, 1.0, 0.0)

    def any_set(f):
        return jnp.max(f) > 0.0

    enough = count_ge(jnp.full_like(top_key, KEY_HALF_NEG + 1)) >= k
    top_tie = enough & (count_ge(top_key) >= k)
    lo = jnp.where(top_tie, top_key, INT_MIN)
    hi = top_key
    done = top_tie | ~enough

    def gallop_cond(s):
        return (s[0] < 5) & any_set(s[3])

    def gallop(s):
        g, lo, hi, want = s
        need_lo = want > 0.5
        step = jnp.left_shift(jnp.int32(1 << 23), 2 * g)
        cand = jnp.where((g < 4) & (top_key >= INT_MIN + step), top_key - step, INT_MIN)
        ok = count_ge(cand) >= k
        lo = jnp.where(need_lo & ok, cand, lo)
        hi = jnp.where(need_lo & ~ok, cand, hi)
        return g + 1, lo, hi, flags(need_lo & ~ok)

    _, lo, hi, _ = lax.while_loop(gallop_cond, gallop, (jnp.int32(0), lo, hi, flags(~done)))

    def bisect_cond(s):
        return any_set(s[2])

    def bisect(s):
        lo, hi, open_f = s
        open_ = open_f > 0.5
        mid = lo + lax.shift_right_logical(hi - lo, jnp.full_like(lo, 1))
        c = count_ge(mid)
        ge = c >= k
        lo = jnp.where(open_ & ge, mid, lo)
        hi = jnp.where(open_ & ~ge, mid, hi)
        return lo, hi, flags(open_ & (c != k) & (hi - lo != 1))

    lo, _, _ = lax.while_loop(bisect_cond, bisect, (lo, hi, flags(~done & (hi - lo != 1))))
    return lo


def _tile_rows(j):
    return pl.ds(pl.multiple_of(j * TILE_K, TILE_K), TILE_K)


def _chunk_mask_t(j, q_chunk):
    k_idx = j * TILE_K + lax.broadcasted_iota(I32, (TILE_K, 1), 0)
    return jnp.where(k_idx >= FRONT_PAD, k_idx >> 6, BIG_CHUNK) <= q_chunk


def _flash_update_t(t_ref, m_ref, acc_ref, vt_tiles, width):
    chunks = [slice(r, r + SOFTMAX_ROWS) for r in range(0, TILE_K, SOFTMAX_ROWS)]
    for b in range(t_ref.shape[1] // width):
        cols = slice(b * width, (b + 1) * width)
        m_prev = m_ref[:, cols]
        m_new = m_prev
        for rows in chunks:
            m_new = jnp.maximum(m_new, jnp.max(t_ref[rows, cols], axis=0, keepdims=True))
        p = jnp.concatenate([jnp.exp2(t_ref[rows, cols] - m_new).astype(BF16) for rows in chunks], axis=0)
        pv = _dot(vt_tiles[0], p[0:TILE_ROWS])
        for c in range(1, VT_PER_K):
            pv = pv + _dot(vt_tiles[c], p[c * TILE_ROWS:(c + 1) * TILE_ROWS])
        acc_ref[:, cols] = jnp.exp2(m_prev - m_new) * acc_ref[:, cols] + pv
        m_ref[:, cols] = m_new


def _params(*sem):
    return pltpu.CompilerParams(dimension_semantics=sem, vmem_limit_bytes=VMEM_LIMIT)


def _ln_kernel(x_ref, g_ref, b_ref, o_ref):
    o_ref[...] = _layernorm(x_ref[...], g_ref[...], b_ref[...])


def _ln_rows(x, g, b):
    n, d = x.shape
    row = pl.BlockSpec((TILE_ROWS, d), lambda i: (i, 0))
    vec = pl.BlockSpec((1, d), lambda i: (0, 0))
    return pl.pallas_call(
        _ln_kernel, grid=(n // TILE_ROWS,), in_specs=[row, vec, vec], out_specs=row,
        out_shape=jax.ShapeDtypeStruct((n, d), F32), compiler_params=_params("parallel"),
        name="ln_in")(x, g.reshape(1, d), b.reshape(1, d))


def _proj_kernel(h_ref, w_ref, c64_ref, s64_ref, c128_ref, s128_ref,
                 qa_ref, qat_ref, kaf_ref, kab_ref, vaf_ref, vab_ref, vat_ref, qb_ref, qbt_ref,
                 kbf_ref, kbb_ref, vbf_ref, vbb_ref, vbt_ref, qi_ref, qit_ref, kiwi_ref, ki_ref,
                 ki4_ref, wt_ref):
    hb = h_ref[...].astype(BF16)
    c64, s64, c128, s128 = c64_ref[...], s64_ref[...], c128_ref[...], s128_ref[...]
    lane = lax.broadcasted_iota(I32, c64.shape, 1)
    low_half = (lane & (A_HEAD_DIM // 2)) == 0

    def rope64(y):
        swapped = jnp.where(low_half, pltpu.roll(y, LANES - 32, 1), pltpu.roll(y, 32, 1))
        return y * c64 + swapped * s64

    def rope128(y):
        return y * c128 + pltpu.roll(y, 64, 1) * s128

    def proj(col, width):
        return _dot(hb, w_ref[:, col:col + width])

    def tiles(y):
        return [y[:, t * LANES:(t + 1) * LANES] for t in range(y.shape[1] // LANES)]

    def put_values_t(vt_ref, y):
        for t, v in enumerate(tiles(y)):
            vt_ref[t, 0:LANES, :] = v.T.astype(BF16)
            vt_ref[t, LANES:V_ROWS, :] = jnp.ones((V_ROWS - LANES, v.shape[0]), BF16)

    for t, y in enumerate(tiles(proj(COL_QA, 1024))):
        r = rope64(y)
        qa_ref[:, t * LANES:(t + 1) * LANES] = r.astype(BF16)
        qat_ref[t * LANES:(t + 1) * LANES, :] = (r * (A_HEAD_DIM ** -0.5 * LOG2E)).T.astype(BF16)
    for t, y in enumerate(tiles(proj(COL_KA, 1024))):
        r = rope64(y)
        kaf_ref[:, t * LANES:(t + 1) * LANES] = r
        kab_ref[:, t * LANES:(t + 1) * LANES] = r.astype(BF16)
    y = proj(COL_VA, 1024)
    vaf_ref[...] = y
    vab_ref[...] = y.astype(BF16)
    put_values_t(vat_ref, y)
    for t, y in enumerate(tiles(proj(COL_QB, 1024))):
        r = rope128(y)
        qb_ref[:, t * LANES:(t + 1) * LANES] = r.astype(BF16)
        qbt_ref[t * LANES:(t + 1) * LANES, :] = (r * (B_HEAD_DIM ** -0.5 * LOG2E)).T.astype(BF16)
    for t, y in enumerate(tiles(proj(COL_KB, 256))):
        r = rope128(y)
        kbf_ref[:, t * LANES:(t + 1) * LANES] = r
        kbb_ref[:, t * LANES:(t + 1) * LANES] = r.astype(BF16)
    y = proj(COL_VB, 256)
    vbf_ref[...] = y
    vbb_ref[...] = y.astype(BF16)
    put_values_t(vbt_ref, y)
    for t, y in enumerate(tiles(proj(COL_QI, 256))):
        r = rope64(y)
        qi_ref[:, t * LANES:(t + 1) * LANES] = r.astype(BF16)
        qit_ref[t * LANES:(t + 1) * LANES, :] = r.T.astype(BF16)
    y = proj(COL_KIWI, LANES)
    is_key = lane < IDX_DIM
    kiwi = jnp.where(is_key, rope64(y), y * (IDX_HEADS ** -0.5))
    kiwi_ref[...] = kiwi
    ki_ref[...] = kiwi[:, 0:IDX_DIM].astype(BF16)
    ki2 = jnp.where(is_key, kiwi, pltpu.roll(kiwi, IDX_DIM, 1)).astype(BF16)
    ki4_ref[:, 0:LANES] = ki2
    ki4_ref[:, LANES:2 * LANES] = ki2
    wt_ref[...] = kiwi.T[IDX_DIM:IDX_DIM + 8, :]


def _project(h, w1, tabs):
    n = h.shape[0]

    def row(width):
        return pl.BlockSpec((TILE_ROWS, width), lambda i: (i, 0))

    def col(height):
        return (pl.BlockSpec((height, TILE_ROWS), lambda i: (0, i)), (height, n))

    def vals(heads):
        return (pl.BlockSpec((heads, None, V_ROWS, TILE_ROWS), lambda i: (0, i, 0, 0)),
                (heads, n // TILE_ROWS, V_ROWS, TILE_ROWS))

    def rows(width):
        return (row(width), (n, width))

    outs = [rows(1024) + (BF16,), col(1024) + (BF16,), rows(1024) + (F32,), rows(1024) + (BF16,),
            rows(1024) + (F32,), rows(1024) + (BF16,), vals(A_HEADS) + (BF16,),
            rows(1024) + (BF16,), col(1024) + (BF16,), rows(256) + (F32,), rows(256) + (BF16,),
            rows(256) + (F32,), rows(256) + (BF16,), vals(B_KV_HEADS) + (BF16,),
            rows(256) + (BF16,), col(256) + (BF16,), rows(LANES) + (F32,), rows(IDX_DIM) + (BF16,),
            rows(256) + (BF16,), col(8) + (F32,)]
    return pl.pallas_call(
        _proj_kernel, grid=(n // TILE_ROWS,),
        in_specs=[row(D_MODEL), pl.BlockSpec((D_MODEL, N_PROJ), lambda i: (0, 0))] + [row(LANES)] * 4,
        out_specs=[spec for spec, _, _ in outs],
        out_shape=[jax.ShapeDtypeStruct(shape, dt) for _, shape, dt in outs],
        compiler_params=_params("parallel"), name="in_proj")(h, w1, *tabs)


def _subln(o1, o2, lam, g, lam_init):
    o = o1 - lam * o2
    y = o * lax.rsqrt(jnp.mean(o * o, -1, keepdims=True) + LN_EPS) * g
    return y * (1.0 - lam_init)


def _diff_p_kernel(lam_ref, qt_ref, k_ref, vt_ref, g_ref, o_ref, qq_sc, ta_sc, tb_sc, m_sc, acc_sc, *,
                   lam_init):
    tq = qt_ref.shape[1]
    i = pl.program_id(1)
    qt = qt_ref[...]
    row = lax.broadcasted_iota(I32, (LANES, 1), 0)
    zero = jnp.zeros_like(qt)
    qq_sc[...] = jnp.concatenate([jnp.where(row < A_HEAD_DIM, qt, zero),
                                  jnp.where(row >= A_HEAD_DIM, qt, zero)], axis=1)
    q_pos = i * tq + lax.broadcasted_iota(I32, (1, tq), 1)
    q_chunk = jnp.concatenate([q_pos, q_pos], axis=1) >> 6
    m_sc[...] = jnp.full(m_sc.shape, NEG_INF, F32)
    acc_sc[...] = jnp.zeros(acc_sc.shape, F32)
    n_kv = ((i + 1) * tq + TILE_K - 1) // TILE_K

    def scores(j):
        return _dot(k_ref[_tile_rows(j), :], qq_sc[...])

    def update(t_ref, j):
        _flash_update_t(t_ref, m_sc, acc_sc, [vt_ref[VT_PER_K * j + c] for c in range(VT_PER_K)], tq)

    def masked_scores(j):
        return jnp.where(_chunk_mask_t(j, q_chunk), scores(j), NEG_INF)

    last = n_kv - 1
    ta_sc[...] = masked_scores(0)

    @pl.when(n_kv == 1)
    def _():
        update(ta_sc, 0)

    def pair(j):
        tb_sc[...] = scores(j + 1)
        update(ta_sc, j)
        ta_sc[...] = scores(j + 2)
        update(tb_sc, j + 1)

    n_octs = jnp.maximum(n_kv - 2, 0) // 8

    def oct_(r, carry):
        for u in range(4):
            pair(8 * r + 2 * u)
        return carry

    lax.fori_loop(0, n_octs, oct_, 0)
    n_quads = jnp.maximum(n_kv - 2, 0) // 4

    def quad(r, carry):
        pair(4 * r)
        pair(4 * r + 2)
        return carry

    lax.fori_loop(2 * n_octs, n_quads, quad, 0)
    extra_pair = (n_kv > 1) & (last - 4 * n_quads >= 3)

    @pl.when(extra_pair)
    def _():
        pair(4 * n_quads)

    j_cur = 4 * n_quads + 2 * extra_pair.astype(I32)

    @pl.when((n_kv > 1) & (last - j_cur == 1))
    def _():
        tb_sc[...] = masked_scores(last)
        update(ta_sc, j_cur)
        update(tb_sc, last)

    @pl.when((n_kv > 1) & (last - j_cur == 2))
    def _():
        tb_sc[...] = scores(j_cur + 1)
        update(ta_sc, j_cur)
        ta_sc[...] = masked_scores(last)
        update(tb_sc, j_cur + 1)
        update(ta_sc, last)

    acc = acc_sc[...]
    outs = [acc[0:LANES, mp * tq:(mp + 1) * tq] / acc[LANES:LANES + 1, mp * tq:(mp + 1) * tq] for mp in range(2)]
    o = outs[0] - lam_ref[0] * outs[1]
    y = o * lax.rsqrt(jnp.mean(o * o, axis=0, keepdims=True) + LN_EPS) * g_ref[...] * (1.0 - lam_init)
    o_ref[...] = y.T.astype(BF16)


def _diff_attn_prompt(lam, qat, ka, vat, g_col, lp, lam_init):
    n = ka.shape[0]
    return pl.pallas_call(
        functools.partial(_diff_p_kernel, lam_init=lam_init),
        grid=(A_HEADS, lp // TILE_Q_A),
        in_specs=[pl.BlockSpec(memory_space=pltpu.SMEM),
                  pl.BlockSpec((LANES, TILE_Q_A), lambda h, i: (h, i)),
                  pl.BlockSpec((lp, LANES), lambda h, i: (0, h)),
                  pl.BlockSpec((None, lp // TILE_ROWS, V_ROWS, TILE_ROWS), lambda h, i: (h, 0, 0, 0)),
                  pl.BlockSpec((LANES, 1), lambda h, i: (0, 0))],
        out_specs=pl.BlockSpec((TILE_Q_A, LANES), lambda h, i: (i, h)),
        out_shape=jax.ShapeDtypeStruct((n, A_HEADS * LANES), BF16),
        scratch_shapes=[pltpu.VMEM((LANES, 2 * TILE_Q_A), BF16),
                        pltpu.VMEM((TILE_K, 2 * TILE_Q_A), F32), pltpu.VMEM((TILE_K, 2 * TILE_Q_A), F32),
                        pltpu.VMEM((1, 2 * TILE_Q_A), F32), pltpu.VMEM((V_ROWS, 2 * TILE_Q_A), F32)],
        compiler_params=_params("parallel", "parallel"), name="diff_attn_prompt")(lam, qat, ka, vat, g_col)


def _diff_s_kernel(lam_ref, q_ref, ck_ref, cv_ref, kn_ref, vn_ref, g_ref, prev_ref, o_ref, *, lam_init, n_keys):
    del prev_ref
    ds_ = q_ref.shape[0]
    past = ck_ref.shape[0]
    pad = n_keys - past - ds_
    lane = lax.broadcasted_iota(I32, (1, LANES), 1)
    valid = lax.broadcasted_iota(I32, (1, n_keys), 1) < past + ds_
    for h in range(A_HEADS):
        cols = slice(h * LANES, (h + 1) * LANES)
        q = q_ref[:, cols]
        zero = jnp.zeros_like(q)
        k = jnp.concatenate([ck_ref[:, cols].astype(BF16), kn_ref[:, cols], jnp.zeros((pad, LANES), BF16)], 0)
        v = jnp.concatenate([cv_ref[:, cols].astype(BF16), vn_ref[:, cols], jnp.zeros((pad, LANES), BF16)], 0)
        qz = jnp.concatenate([jnp.where(lane < A_HEAD_DIM, q, zero), jnp.where(lane >= A_HEAD_DIM, q, zero)], 0)
        s = jnp.where(valid, _dot_nt(qz, k) * (A_HEAD_DIM ** -0.5), NEG_INF)
        p = jnp.exp(s - jnp.max(s, axis=1, keepdims=True))
        o = _dot(p.astype(BF16), v) / jnp.sum(p, axis=1, keepdims=True)
        o_ref[:, cols] = _subln(o[0:ds_], o[ds_:2 * ds_], lam_ref[0], g_ref[...], lam_init).astype(BF16)


def _diff_attn_sample(lam, qa, cache_k, cache_v, ka, va, g, ya, layer, lp, lam_init):
    _, nb, past, width = cache_k.shape
    n = qa.shape[0]
    ds_ = (n - lp) // nb
    n_keys = -(-(past + ds_) // LANES) * LANES
    base = lp // ds_
    new = pl.BlockSpec((ds_, width), lambda b: (base + b, 0))
    cache = pl.BlockSpec((None, None, past, width), lambda b: (layer, b, 0, 0))
    return pl.pallas_call(
        functools.partial(_diff_s_kernel, lam_init=lam_init, n_keys=n_keys), grid=(nb,),
        in_specs=[pl.BlockSpec(memory_space=pltpu.SMEM), new, cache, cache, new, new,
                  pl.BlockSpec((1, LANES), lambda b: (0, 0)), pl.BlockSpec(memory_space=pl.ANY)],
        out_specs=new, out_shape=jax.ShapeDtypeStruct(ya.shape, ya.dtype),
        input_output_aliases={7: 0},
        compiler_params=_params("parallel"), name="diff_attn_sample")(lam, qa, cache_k, cache_v, ka, va, g, ya)


def _index_scores(qiz, w_cols, ki4, rows):
    r = _dot_nt(qiz, ki4)
    sc = w_cols[0] * jnp.maximum(r[0:rows], 0.0)
    for h in range(1, IDX_HEADS):
        sc = sc + w_cols[h] * jnp.maximum(r[h * rows:(h + 1) * rows], 0.0)
    return sc


def _stack_idx_queries(qi, rows):
    lane = lax.broadcasted_iota(I32, (1, IDX_HEADS * IDX_DIM), 1)
    zero = jnp.zeros_like(qi)
    return [jnp.where((lane >= IDX_DIM * h) & (lane < IDX_DIM * (h + 1)), qi, zero) for h in range(IDX_HEADS)]


def _selected(key, t, need, seen, u):
    eq = key == t
    prefix = _dot(jnp.where(eq, 1.0, 0.0).astype(BF16), u)
    sel = ((key > t) | (eq & (seen + prefix <= need))) & (key > KEY_HALF_NEG)
    return sel, seen + prefix[:, -1:]


def _dsa_p_kernel(qbt_ref, qit_ref, wt_ref, kb_ref, vbt_ref, ki_ref, tri_ref, o_ref,
                  key_sc, qq_sc, ta_sc, tb_sc, sel_sc, m_sc, acc_sc, *, n_sel):
    tq = qbt_ref.shape[1]
    i = pl.program_id(0)
    n_kv = ((i + 1) * tq + TILE_K - 1) // TILE_K
    w = wt_ref[...] * (IDX_DIM ** -0.5)
    w_rows = [w[h:h + 1] for h in range(IDX_HEADS)]
    q_chunk = (i * tq + lax.broadcasted_iota(I32, (1, tq), 1)) >> 6

    last = n_kv - 1

    def put_products(t_ref, j):
        ki = ki_ref[_tile_rows(j), :]
        for h in range(IDX_HEADS):
            t_ref[:, h * tq:(h + 1) * tq] = _dot(ki, qit_ref[h * IDX_DIM:(h + 1) * IDX_DIM, :])

    def put_keys(t_ref, j, top):
        sc = None
        for h in range(IDX_HEADS):
            r = w_rows[h] * jnp.maximum(t_ref[:, h * tq:(h + 1) * tq], 0.0)
            sc = r if sc is None else sc + r
        key = _sort_key(jnp.where(_chunk_mask_t(j, q_chunk), sc, NEG_INF))
        key_sc[j] = key
        return jnp.maximum(top, jnp.max(key, axis=0, keepdims=True))

    put_products(ta_sc, 0)

    def score_pair(r, top):
        j = 2 * r
        j1 = jnp.minimum(j + 1, last)
        put_products(tb_sc, j1)
        top = put_keys(ta_sc, j, top)
        put_products(ta_sc, jnp.minimum(j + 2, last))
        return put_keys(tb_sc, j1, top)

    top_key = lax.fori_loop(0, (n_kv + 1) // 2, score_pair, jnp.full((1, tq), INT_MIN, I32))

    def count_ge(c):
        def body(j, acc):
            parts = [None] * COUNT_CHAINS
            for n, r in enumerate(range(0, TILE_K, COUNT_ROWS)):
                hit = jnp.where(key_sc[j, r:r + COUNT_ROWS, :] >= c, 1.0, 0.0)
                parts[n % COUNT_CHAINS] = hit if parts[n % COUNT_CHAINS] is None else parts[n % COUNT_CHAINS] + hit
            return acc + functools.reduce(lambda a, b: a + b, parts)
        return jnp.sum(lax.fori_loop(0, n_kv, body, jnp.zeros((COUNT_ROWS, tq), F32)), axis=0, keepdims=True)

    thr = _kth_largest_key_bracketed(count_ge, top_key, float(n_sel))
    need = float(n_sel) - count_ge(thr + 1)
    need = jnp.where(thr > KEY_HALF_NEG, need, 0.0)
    floor_key = jnp.maximum(thr, KEY_HALF_NEG)

    for c in range(B_KV_HEADS):
        for g in range(B_GROUP):
            hh = c * B_GROUP + g
            qq_sc[c, :, g * tq:(g + 1) * tq] = qbt_ref[hh * LANES:(hh + 1) * LANES, :]
    m_sc[...] = jnp.full(m_sc.shape, NEG_INF, F32)
    acc_sc[...] = jnp.zeros(acc_sc.shape, F32)

    def put_scores(t_ref, j, c):
        s = _dot(kb_ref[_tile_rows(j), c * LANES:(c + 1) * LANES], qq_sc[c])
        sel = sel_sc[...] > 0.5
        for g in range(B_GROUP):
            t_ref[:, g * tq:(g + 1) * tq] = jnp.where(sel, s[:, g * tq:(g + 1) * tq], NEG_INF)

    def update(t_ref, j, c):
        _flash_update_t(t_ref, m_sc.at[c], acc_sc.at[c],
                        [vbt_ref[c, VT_PER_K * j + cc] for cc in range(VT_PER_K)], tq)

    def select(j, seen):
        key = key_sc[j]
        tie = key == thr
        half = TILE_K // 2
        ones = jnp.where(tie, 1.0, 0.0).astype(BF16)
        for r in range(2):
            rows = slice(r * half, (r + 1) * half)
            prefix = seen + _dot(tri_ref[...], ones[rows])
            sel_sc[rows, :] = jnp.where((key[rows] > floor_key) | (tie[rows] & (prefix <= need)), 1.0, 0.0)
            seen = prefix[half - 1:half, :]
        return seen

    seen0 = select(0, jnp.zeros((1, tq), F32))
    put_scores(ta_sc, 0, 0)

    def attend(j, seen):
        nxt = jnp.minimum(j + 1, last)
        put_scores(tb_sc, j, 1)
        update(ta_sc, j, 0)
        seen = select(nxt, seen)
        put_scores(ta_sc, nxt, 0)
        update(tb_sc, j, 1)
        return seen

    def attend_four(r, seen):
        for u in range(4):
            seen = attend(4 * r + u, seen)
        return seen

    seen_mid = lax.fori_loop(0, n_kv // 4, attend_four, seen0)
    lax.fori_loop(4 * (n_kv // 4), n_kv, attend, seen_mid)
    for c in range(B_KV_HEADS):
        acc = acc_sc[c]
        for g in range(B_GROUP):
            hh = c * B_GROUP + g
            a = acc[:, g * tq:(g + 1) * tq]
            o_ref[:, hh * LANES:(hh + 1) * LANES] = (a[0:LANES] / a[LANES:LANES + 1]).T.astype(BF16)


def _dsa_prompt(qbt, qit, wt, kb, vbt, ki, tri, lp, n_sel):
    n = kb.shape[0]
    tq = TILE_Q_B

    def col(height):
        return pl.BlockSpec((height, tq), lambda i: (0, i))

    def resident(shape):
        return pl.BlockSpec(shape, lambda i: (0,) * len(shape), pipeline_mode=pl.Buffered(1))

    return pl.pallas_call(
        functools.partial(_dsa_p_kernel, n_sel=n_sel), grid=(lp // tq,),
        in_specs=[col(1024), col(256), col(8), resident((lp, 256)),
                  resident((B_KV_HEADS, lp // TILE_ROWS, V_ROWS, TILE_ROWS)), resident((lp, IDX_DIM)),
                  resident((TILE_K // 2, TILE_K // 2))],
        out_specs=pl.BlockSpec((tq, 1024), lambda i: (i, 0)),
        out_shape=jax.ShapeDtypeStruct((n, 1024), BF16),
        scratch_shapes=[pltpu.VMEM((lp // TILE_K, TILE_K, tq), I32),
                        pltpu.VMEM((B_KV_HEADS, LANES, B_GROUP * tq), BF16),
                        pltpu.VMEM((TILE_K, B_GROUP * tq), F32), pltpu.VMEM((TILE_K, B_GROUP * tq), F32),
                        pltpu.VMEM((TILE_K, tq), F32),
                        pltpu.VMEM((B_KV_HEADS, 1, B_GROUP * tq), F32),
                        pltpu.VMEM((B_KV_HEADS, V_ROWS, B_GROUP * tq), F32)],
        compiler_params=_params("parallel"), name="dsa_prompt")(qbt, qit, wt, kb, vbt, ki, tri)


def _dsa_s_kernel(qb_ref, qi_ref, kiwi_ref, ckb_ref, cvb_ref, cki_ref, kbn_ref, vbn_ref, ki4n_ref,
                  rep_ref, u_ref, prev_ref, o_ref, *, n_sel, n_keys):
    del prev_ref
    tq = qb_ref.shape[0]
    past = ckb_ref.shape[0]
    pad = n_keys - past - tq
    width = B_KV_HEADS * B_HEAD_DIM
    kb = jnp.concatenate([ckb_ref[...].astype(BF16), kbn_ref[...], jnp.zeros((pad, width), BF16)], 0)
    vb = jnp.concatenate([cvb_ref[...].astype(BF16), vbn_ref[...], jnp.zeros((pad, width), BF16)], 0)
    ki4_past = _dot(cki_ref[...].astype(BF16), rep_ref[...]).astype(BF16)
    ki4 = jnp.concatenate([ki4_past, ki4n_ref[...], jnp.zeros((pad, width), BF16)], 0)
    qiz = jnp.concatenate(_stack_idx_queries(qi_ref[...], tq), 0)
    wq = kiwi_ref[...] * (IDX_DIM ** -0.5)
    w_cols = [wq[:, IDX_DIM + h:IDX_DIM + h + 1] for h in range(IDX_HEADS)]
    sc = _index_scores(qiz, w_cols, ki4, tq)
    valid = lax.broadcasted_iota(I32, (1, n_keys), 1) < past + tq
    key = _sort_key(jnp.where(valid, sc, NEG_INF))

    def count_ge(c):
        return jnp.sum(jnp.where(key >= c, 1.0, 0.0), axis=1, keepdims=True)

    t = _kth_largest_key(count_ge, (tq, 1), float(n_sel))
    need = float(n_sel) - count_ge(t + 1)
    sel, _ = _selected(key, t, need, jnp.zeros((tq, 1), F32), u_ref[...])
    for c in range(B_KV_HEADS):
        cols = slice(c * LANES, (c + 1) * LANES)
        q = jnp.concatenate([qb_ref[:, (c * B_GROUP + g) * LANES:(c * B_GROUP + g + 1) * LANES]
                             for g in range(B_GROUP)], 0)
        s_all = _dot_nt(q, kb[:, cols]) * (B_HEAD_DIM ** -0.5)
        for g in range(B_GROUP):
            hh = c * B_GROUP + g
            s = jnp.where(sel, s_all[g * tq:(g + 1) * tq], NEG_INF)
            p = jnp.exp(s - jnp.max(s, axis=1, keepdims=True))
            o = _dot(p.astype(BF16), vb[:, cols]) / jnp.sum(p, axis=1, keepdims=True)
            o_ref[:, hh * LANES:(hh + 1) * LANES] = o.astype(BF16)


def _dsa_sample(qb, qi, kiwi, cache_k, cache_v, cache_idx, kb, vb, ki4, rep, u, yb, layer, lp, n_sel):
    _, nb, past, _ = cache_k.shape
    n = qb.shape[0]
    ds_ = (n - lp) // nb
    n_keys = u.shape[0]
    base = lp // ds_

    def new(width):
        return pl.BlockSpec((ds_, width), lambda b: (base + b, 0))

    def cache(width):
        return pl.BlockSpec((None, None, past, width), lambda b: (layer, b, 0, 0))

    def const(shape):
        return pl.BlockSpec(shape, lambda b: (0, 0))

    return pl.pallas_call(
        functools.partial(_dsa_s_kernel, n_sel=n_sel, n_keys=n_keys), grid=(nb,),
        in_specs=[new(1024), new(256), new(LANES), cache(256), cache(256), cache(IDX_DIM),
                  new(256), new(256), new(256), const(rep.shape), const(u.shape),
                  pl.BlockSpec(memory_space=pl.ANY)],
        out_specs=new(1024), out_shape=jax.ShapeDtypeStruct(yb.shape, yb.dtype),
        input_output_aliases={11: 0},
        compiler_params=_params("parallel"), name="dsa_sample")(
            qb, qi, kiwi, cache_k, cache_v, cache_idx, kb, vb, ki4, rep, u, yb)


def _merge_kernel(h_ref, ya_ref, yb_ref, wga_ref, wgb_ref, wpa_ref, wpb_ref, wo_ref, g_ref, b_ref,
                  wr_ref, bias_ref, o_ref, gates_ref):
    h = h_ref[...]
    hb = h.astype(BF16)
    mixed = (jax.nn.sigmoid(_dot(hb, wga_ref[...])) * _dot(ya_ref[...], wpa_ref[...])
             + jax.nn.sigmoid(_dot(hb, wgb_ref[...])) * _dot(yb_ref[...], wpb_ref[...]))
    y = _dot(mixed.astype(BF16), wo_ref[...])
    h1 = _layernorm(DEEPNORM_ALPHA * h + y, g_ref[...], b_ref[...])
    o_ref[...] = h1
    gates_ref[...] = _route_t(jax.nn.sigmoid(_dot_nt(wr_ref[...], h1.astype(BF16))), bias_ref[...])


def _merge(h, ya, yb, wga, wgb, wpa, wpb, wo, g, b, wr_t, bias):
    n, d = h.shape
    row = pl.BlockSpec((TILE_ROWS, d), lambda i: (i, 0))
    mat = pl.BlockSpec((d, d), lambda i: (0, 0))
    vec = pl.BlockSpec((1, d), lambda i: (0, 0))
    return pl.pallas_call(
        _merge_kernel, grid=(n // TILE_ROWS,),
        in_specs=[row, row, row] + [mat] * 5 + [vec, vec, pl.BlockSpec((N_EXPERTS, d), lambda i: (0, 0)),
                                                 pl.BlockSpec((N_EXPERTS, 1), lambda i: (0, 0))],
        out_specs=[row, pl.BlockSpec((N_EXPERTS, TILE_ROWS), lambda i: (0, i))],
        out_shape=[jax.ShapeDtypeStruct((n, d), F32), jax.ShapeDtypeStruct((N_EXPERTS, n), F32)],
        compiler_params=_params("parallel"), name="merge_ln1_route")(
            h, ya, yb, wga, wgb, wpa, wpb, wo, g.reshape(1, d), b.reshape(1, d), wr_t,
            bias.reshape(N_EXPERTS, 1))


def _route_t(s, bias):
    sel = s + bias
    row_f = lax.broadcasted_iota(I32, (N_EXPERTS, 1), 0).astype(F32)
    neg = -jnp.inf

    def first_argmax(x, m, rows):
        return jnp.min(jnp.where(x == m, rows, float(N_EXPERTS)), axis=0, keepdims=True)

    groups = [slice(g * GROUP_SIZE, (g + 1) * GROUP_SIZE) for g in range(N_GROUPS)]
    g_score = []
    for rows in groups:
        x = sel[rows]
        m1 = jnp.max(x, axis=0, keepdims=True)
        x = jnp.where(row_f[rows] == first_argmax(x, m1, row_f[rows]), neg, x)
        g_score.append(m1 + jnp.max(x, axis=0, keepdims=True))
    kept = []
    for g, rows in enumerate(groups):
        ahead = jnp.zeros_like(g_score[g])
        for o in range(N_GROUPS):
            if o != g:
                beats = (g_score[o] > g_score[g]) | ((g_score[o] == g_score[g]) & (o < g))
                ahead = ahead + jnp.where(beats, 1.0, 0.0)
        kept.append(jnp.where(ahead < float(TOPK_GROUPS), sel[rows], NEG_INF))
    x = jnp.concatenate(kept, axis=0)
    chosen = None
    for _ in range(TOP_K):
        hit = row_f == first_argmax(x, jnp.max(x, axis=0, keepdims=True), row_f)
        chosen = hit if chosen is None else (chosen | hit)
        x = jnp.where(hit, neg, x)
    w = jnp.where(chosen, s, 0.0)
    return w / jnp.sum(w, axis=0, keepdims=True) * ROUTE_SCALE


def _silu(x):
    return x * jax.nn.sigmoid(x)


def _moe_kernel(h_ref, gates_ref, wg_ref, wu_ref, wd_ref, sg_ref, su_ref, sd_ref, g_ref, b_ref, o_ref,
                xb_sc, acc_sc):
    step = pl.program_id(1)

    @pl.when(step == 0)
    def _():
        xb = h_ref[...].astype(BF16)
        xb_sc[...] = xb
        a = _silu(_dot(xb, sg_ref[...])) * _dot(xb, su_ref[...])
        acc_sc[...] = _dot(a.astype(BF16), sd_ref[...])

    xb = xb_sc[...]
    g_rows = gates_ref[...]
    gates = jnp.concatenate([g_rows, jnp.zeros((LANES - EXPERT_BLOCK, g_rows.shape[1]), F32)], axis=0).T
    acts = []
    for e in range(EXPERT_BLOCK):
        a = _silu(_dot(xb, wg_ref[e])) * _dot(xb, wu_ref[e]) * gates[:, e:e + 1]
        acts.append(a.astype(BF16))
    wd = wd_ref[...]
    acc_sc[...] += _dot(jnp.concatenate(acts, axis=1), wd.reshape(wd.shape[0] * wd.shape[1], wd.shape[2]))

    @pl.when(step == pl.num_programs(1) - 1)
    def _():
        o_ref[...] = _layernorm(DEEPNORM_ALPHA * h_ref[...] + acc_sc[...], g_ref[...], b_ref[...])


def _moe(h, gates, wg, wu, wd, sg, su, sd, g, b):
    n, d = h.shape
    f = wg.shape[-1]
    row = pl.BlockSpec((TILE_MOE, d), lambda r, e: (r, 0))
    vec = pl.BlockSpec((1, d), lambda r, e: (0, 0))
    return pl.pallas_call(
        _moe_kernel, grid=(n // TILE_MOE, N_EXPERTS // EXPERT_BLOCK),
        in_specs=[row, pl.BlockSpec((EXPERT_BLOCK, TILE_MOE), lambda r, e: (e, r)),
                  pl.BlockSpec((EXPERT_BLOCK, d, f), lambda r, e: (e, 0, 0)),
                  pl.BlockSpec((EXPERT_BLOCK, d, f), lambda r, e: (e, 0, 0)),
                  pl.BlockSpec((EXPERT_BLOCK, f, d), lambda r, e: (e, 0, 0)),
                  pl.BlockSpec((d, sg.shape[1]), lambda r, e: (0, 0)),
                  pl.BlockSpec((d, sg.shape[1]), lambda r, e: (0, 0)),
                  pl.BlockSpec((sg.shape[1], d), lambda r, e: (0, 0)), vec, vec],
        out_specs=row, out_shape=jax.ShapeDtypeStruct((n, d), F32),
        scratch_shapes=[pltpu.VMEM((TILE_MOE, d), BF16), pltpu.VMEM((TILE_MOE, d), F32)],
        compiler_params=_params("parallel", "arbitrary"), name="moe_ln2")(
            h, gates, wg, wu, wd, sg, su, sd, g.reshape(1, d), b.reshape(1, d))


def _rope_tables(pos):
    def table(half, reps):
        inv = ROPE_THETA ** (-jnp.arange(half, dtype=F32) / half)
        ang = pos.astype(F32)[:, None] * inv[None, :]
        cos, sin = jnp.cos(ang), jnp.sin(ang)
        return (jnp.tile(jnp.concatenate([cos, cos], -1), (1, reps)),
                jnp.tile(jnp.concatenate([-sin, sin], -1), (1, reps)))
    c64, s64 = table(A_HEAD_DIM // 2, 2)
    c128, s128 = table(B_HEAD_DIM // 2, 1)
    return c64, s64, c128, s128


def kernel(x_prompt, x_sample, cache_a_k, cache_a_v, cache_b_k, cache_b_v, cache_b_idx, meta_tokens, ln_in_g, ln_in_b, w_in, lam_q1, lam_k1, lam_q2, lam_k2, subln_g, w_proj_a, w_proj_b, w_out, ln1_g, ln1_b, w_router, router_bias, w_expert_gate, w_expert_up, w_expert_down, w_shared_gate, w_shared_up, w_shared_down, ln2_g, ln2_b):
    batch, seq, d = x_prompt.shape
    nb, ds_, _ = x_sample.shape
    depth, _, past = cache_a_k.shape[:3]
    assert batch == 1 and d == D_MODEL and seq % TILE_K == 0 and ds_ == CHUNK
    lp = FRONT_PAD + N_META + seq
    ns = nb * ds_
    n = lp + ns
    assert n % TILE_MOE == 0 and past % 16 == 0
    n_sel_p = min(TOPK_MAX, seq // 4)
    n_sel_s = min(TOPK_MAX, (past + ds_) // 4)
    n_keys_s = -(-(past + ds_) // LANES) * LANES

    x_all = jnp.concatenate([jnp.zeros((FRONT_PAD, d), F32), meta_tokens.astype(F32), x_prompt[0],
                             x_sample.reshape(ns, d)], 0)
    pos = jnp.concatenate([jnp.maximum(jnp.arange(lp, dtype=jnp.int32) - FRONT_PAD, 0),
                           jnp.tile(past + jnp.arange(ds_, dtype=jnp.int32), nb)])
    tabs = _rope_tables(pos)
    tri_p = jnp.tril(jnp.ones((TILE_K // 2, TILE_K // 2), BF16))
    u_s = jnp.triu(jnp.ones((n_keys_s, n_keys_s), BF16))
    rep = jnp.tile(jnp.eye(IDX_DIM, dtype=BF16), (1, IDX_HEADS))
    ck_a = cache_a_k.reshape(depth, nb, past, -1)
    cv_a = cache_a_v.reshape(depth, nb, past, -1)
    ck_b = cache_b_k.reshape(depth, nb, past, -1)
    cv_b = cache_b_v.reshape(depth, nb, past, -1)

    h = _ln_rows(x_all, ln_in_g, ln_in_b)
    rows = []
    for l in range(depth):
        lam_init = 0.8 - 0.6 * math.exp(-0.3 * l)
        lam = (jnp.exp(jnp.sum(lam_q1[l].astype(F32) * lam_k1[l].astype(F32)))
               - jnp.exp(jnp.sum(lam_q2[l].astype(F32) * lam_k2[l].astype(F32))) + lam_init).reshape(1)
        w1 = jnp.concatenate([w_in[l][:, :N_RAW_PROJ], jnp.zeros((d, N_PROJ - N_RAW_PROJ), F32)], 1).astype(BF16)
        wga = w_in[l][:, N_RAW_PROJ:N_RAW_PROJ + d].astype(BF16)
        wgb = w_in[l][:, N_RAW_PROJ + d:].astype(BF16)
        (qa, qat, ka_f, ka, va_f, va, vat, qb, qbt, kb_f, kb, vb_f, vb, vbt, qi, qit, kiwi, ki, ki4,
         wt) = _project(h, w1, tabs)
        rows.append((ka_f, va_f, kb_f, vb_f, kiwi))
        g = subln_g[l].astype(F32)
        ya = _diff_attn_prompt(lam, qat, ka, vat, g.reshape(LANES, 1), lp, lam_init)
        ya = _diff_attn_sample(lam, qa, ck_a, cv_a, ka, va, g.reshape(1, LANES), ya, l, lp, lam_init)
        yb = _dsa_prompt(qbt, qit, wt, kb, vbt, ki, tri_p, lp, n_sel_p)
        yb = _dsa_sample(qb, qi, kiwi, ck_b, cv_b, cache_b_idx, kb, vb, ki4, rep, u_s, yb, l, lp, n_sel_s)
        h, gates = _merge(h, ya, yb, wga, wgb, w_proj_a[l].astype(BF16), w_proj_b[l].astype(BF16),
                          w_out[l].astype(BF16), ln1_g[l], ln1_b[l], w_router[l].T.astype(BF16),
                          router_bias[l].astype(F32))
        h = _moe(h, gates, w_expert_gate[l].astype(BF16), w_expert_up[l].astype(BF16),
                 w_expert_down[l].astype(BF16), w_shared_gate[l].astype(BF16), w_shared_up[l].astype(BF16),
                 w_shared_down[l].astype(BF16), ln2_g[l], ln2_b[l])

    p0 = FRONT_PAD
    y_prompt = h[p0 + N_META:lp].reshape(1, seq, d)
    y_sample = h[lp:].reshape(nb, ds_, d)

    def stack(idx, width, tail):
        return (jnp.stack([r[idx][p0:lp, :width] for r in rows]).reshape((depth, 1, lp - p0) + tail),
                jnp.stack([r[idx][lp:, :width] for r in rows]).reshape((depth, nb, ds_) + tail))

    ak_p, ak_s = stack(0, 1024, (2 * A_HEADS, A_HEAD_DIM))
    av_p, av_s = stack(1, 1024, (A_HEADS, 2 * A_HEAD_DIM))
    bk_p, bk_s = stack(2, 256, (B_KV_HEADS, B_HEAD_DIM))
    bv_p, bv_s = stack(3, 256, (B_KV_HEADS, B_HEAD_DIM))
    bi_p, bi_s = stack(4, IDX_DIM, (IDX_DIM,))
    return (y_prompt, y_sample, ak_p, av_p, bk_p, bv_p, bi_p, ak_s, av_s, bk_s, bv_s, bi_s)
```

```python
import functools
import math

import numpy as np
import jax
import jax.numpy as jnp
from jax import lax
from jax.experimental import pallas as pl
from jax.experimental.pallas import tpu as pltpu

F32 = jnp.float32
BF16 = jnp.bfloat16
I32 = jnp.int32

MODEL_DEPTH = 4
CHUNK = 64
N_META = 16
ROPE_THETA = 10000.0
LN_EPS = 1e-5
NEG_INF = -1e30
DEEPNORM_ALPHA = (2 * MODEL_DEPTH) ** 0.25
A_HEADS = 8
A_HEAD_DIM = 64
B_HEADS = 8
B_KV_HEADS = 2
B_GROUP = B_HEADS // B_KV_HEADS
B_HEAD_DIM = 128
IDX_HEADS = 4
IDX_DIM = 64
TOPK_MAX = 256
N_EXPERTS = 64
TOP_K = 8
N_GROUPS = 8
GROUP_SIZE = N_EXPERTS // N_GROUPS
TOPK_GROUPS = 4
ROUTE_SCALE = 2.5

LANES = 128
D_MODEL = 1024
N_PROJ = 4992
COL_QA, COL_KA, COL_VA, COL_QB, COL_KB, COL_VB, COL_QI, COL_KIWI = (
    0, 1024, 2048, 3072, 4096, 4352, 4608, 4864)
N_RAW_PROJ = 4932

TILE_K = 512
FRONT_PAD = TILE_K - N_META
TILE_Q_A = 512
TILE_Q_B = 256
TILE_ROWS = 256
TILE_MOE = 512
ROUTE_ROWS = 64
EXPERT_BLOCK = 8
VMEM_LIMIT = 56 * 1024 * 1024
BIG_CHUNK = 1 << 30
INT_MIN = -(2 ** 31)
LOG2E = 1.4426950408889634
V_ROWS = 144
VT_PER_K = TILE_K // TILE_ROWS
COUNT_ROWS = 16
COUNT_CHAINS = 4
SOFTMAX_ROWS = 256


def _f32_key(v):
    b = int(np.float32(v).view(np.int32))
    return b ^ ((b >> 31) & 0x7FFFFFFF)


KEY_HALF_NEG = _f32_key(0.5 * NEG_INF)


def _dot(a, b):
    return jnp.dot(a, b, preferred_element_type=F32)


def _dot_nt(a, b):
    return lax.dot_general(a, b, (((1,), (1,)), ((), ())), preferred_element_type=F32)


def _layernorm(x, g, b):
    mu = jnp.mean(x, -1, keepdims=True)
    xc = x - mu
    var = jnp.mean(xc * xc, -1, keepdims=True)
    return xc * lax.rsqrt(var + LN_EPS) * g + b


def _sort_key(x):
    bits = lax.bitcast_convert_type(x, I32)
    return bits ^ ((bits >> 31) & 0x7FFFFFFF)


def _kth_largest_key(count_ge, shape, k):
    def body(b, tu):
        cand = tu | jnp.left_shift(jnp.int32(1), 31 - b)
        c = count_ge(cand ^ INT_MIN)
        return jnp.where(c >= k, cand, tu)
    tu = lax.fori_loop(0, 32, body, jnp.zeros(shape, I32))
    return tu ^ INT_MIN


def _kth_largest_key_bracketed(count_ge, top_key, k):
    def flags(b):
        return jnp.where(b, 1.0, 0.0)

    def any_set(f):
        return jnp.max(f) > 0.0

    enough = count_ge(jnp.full_like(top_key, KEY_HALF_NEG + 1)) >= k
    top_tie = enough & (count_ge(top_key) >= k)
    lo = jnp.where(top_tie, top_key, INT_MIN)
    hi = top_key
    done = top_tie | ~enough

    def gallop_cond(s):
        return (s[0] < 5) & any_set(s[3])

    def gallop(s):
        g, lo, hi, want = s
        need_lo = want > 0.5
        step = jnp.left_shift(jnp.int32(1 << 23), 2 * g)
        cand = jnp.where((g < 4) & (top_key >= INT_MIN + step), top_key - step, INT_MIN)
        ok = count_ge(cand) >= k
        lo = jnp.where(need_lo & ok, cand, lo)
        hi = jnp.where(need_lo & ~ok, cand, hi)
        return g + 1, lo, hi, flags(need_lo & ~ok)

    _, lo, hi, _ = lax.while_loop(gallop_cond, gallop, (jnp.int32(0), lo, hi, flags(~done)))

    def bisect_cond(s):
        return any_set(s[2])

    def bisect(s):
        lo, hi, open_f = s
        open_ = open_f > 0.5
        mid = lo + lax.shift_right_logical(hi - lo, jnp.full_like(lo, 1))
        c = count_ge(mid)
        ge = c >= k
        lo = jnp.where(open_ & ge, mid, lo)
        hi = jnp.where(open_ & ~ge, mid, hi)
        return lo, hi, flags(open_ & (c != k) & (hi - lo != 1))

    lo, _, _ = lax.while_loop(bisect_cond, bisect, (lo, hi, flags(~done & (hi - lo != 1))))
    return lo


def _tile_rows(j):
    return pl.ds(pl.multiple_of(j * TILE_K, TILE_K), TILE_K)


def _chunk_mask_t(j, q_chunk):
    k_idx = j * TILE_K + lax.broadcasted_iota(I32, (TILE_K, 1), 0)
    return jnp.where(k_idx >= FRONT_PAD, k_idx >> 6, BIG_CHUNK) <= q_chunk


def _flash_update_t(t_ref, m_ref, acc_ref, vt_tiles, width):
    chunks = [slice(r, r + SOFTMAX_ROWS) for r in range(0, TILE_K, SOFTMAX_ROWS)]
    for b in range(t_ref.shape[1] // width):
        cols = slice(b * width, (b + 1) * width)
        m_prev = m_ref[:, cols]
        m_new = m_prev
        for rows in chunks:
            m_new = jnp.maximum(m_new, jnp.max(t_ref[rows, cols], axis=0, keepdims=True))
        p = jnp.concatenate([jnp.exp2(t_ref[rows, cols] - m_new).astype(BF16) for rows in chunks], axis=0)
        pv = _dot(vt_tiles[0], p[0:TILE_ROWS])
        for c in range(1, VT_PER_K):
            pv = pv + _dot(vt_tiles[c], p[c * TILE_ROWS:(c + 1) * TILE_ROWS])
        acc_ref[:, cols] = jnp.exp2(m_prev - m_new) * acc_ref[:, cols] + pv
        m_ref[:, cols] = m_new


def _params(*sem):
    return pltpu.CompilerParams(dimension_semantics=sem, vmem_limit_bytes=VMEM_LIMIT)


def _ln_kernel(x_ref, g_ref, b_ref, o_ref):
    o_ref[...] = _layernorm(x_ref[...], g_ref[...], b_ref[...])


def _ln_rows(x, g, b):
    n, d = x.shape
    row = pl.BlockSpec((TILE_ROWS, d), lambda i: (i, 0))
    vec = pl.BlockSpec((1, d), lambda i: (0, 0))
    return pl.pallas_call(
        _ln_kernel, grid=(n // TILE_ROWS,), in_specs=[row, vec, vec], out_specs=row,
        out_shape=jax.ShapeDtypeStruct((n, d), F32), compiler_params=_params("parallel"),
        name="ln_in")(x, g.reshape(1, d), b.reshape(1, d))


def _proj_kernel(h_ref, w_ref, c64_ref, s64_ref, c128_ref, s128_ref,
                 qa_ref, qat_ref, kaf_ref, kab_ref, vaf_ref, vab_ref, vat_ref, qb_ref, qbt_ref,
                 kbf_ref, kbb_ref, vbf_ref, vbb_ref, vbt_ref, qi_ref, qit_ref, kiwi_ref, ki_ref,
                 ki4_ref, wt_ref):
    hb = h_ref[...].astype(BF16)
    c64, s64, c128, s128 = c64_ref[...], s64_ref[...], c128_ref[...], s128_ref[...]
    lane = lax.broadcasted_iota(I32, c64.shape, 1)
    low_half = (lane & (A_HEAD_DIM // 2)) == 0

    def rope64(y):
        swapped = jnp.where(low_half, pltpu.roll(y, LANES - 32, 1), pltpu.roll(y, 32, 1))
        return y * c64 + swapped * s64

    def rope128(y):
        return y * c128 + pltpu.roll(y, 64, 1) * s128

    def proj(col, width):
        return _dot(hb, w_ref[:, col:col + width])

    def tiles(y):
        return [y[:, t * LANES:(t + 1) * LANES] for t in range(y.shape[1] // LANES)]

    def put_values_t(vt_ref, y):
        for t, v in enumerate(tiles(y)):
            vt_ref[t, 0:LANES, :] = v.T.astype(BF16)
            vt_ref[t, LANES:V_ROWS, :] = jnp.ones((V_ROWS - LANES, v.shape[0]), BF16)

    for t, y in enumerate(tiles(proj(COL_QA, 1024))):
        r = rope64(y)
        qa_ref[:, t * LANES:(t + 1) * LANES] = r.astype(BF16)
        qat_ref[t * LANES:(t + 1) * LANES, :] = (r * (A_HEAD_DIM ** -0.5 * LOG2E)).T.astype(BF16)
    for t, y in enumerate(tiles(proj(COL_KA, 1024))):
        r = rope64(y)
        kaf_ref[:, t * LANES:(t + 1) * LANES] = r
        kab_ref[:, t * LANES:(t + 1) * LANES] = r.astype(BF16)
    y = proj(COL_VA, 1024)
    vaf_ref[...] = y
    vab_ref[...] = y.astype(BF16)
    put_values_t(vat_ref, y)
    for t, y in enumerate(tiles(proj(COL_QB, 1024))):
        r = rope128(y)
        qb_ref[:, t * LANES:(t + 1) * LANES] = r.astype(BF16)
        qbt_ref[t * LANES:(t + 1) * LANES, :] = (r * (B_HEAD_DIM ** -0.5 * LOG2E)).T.astype(BF16)
    for t, y in enumerate(tiles(proj(COL_KB, 256))):
        r = rope128(y)
        kbf_ref[:, t * LANES:(t + 1) * LANES] = r
        kbb_ref[:, t * LANES:(t + 1) * LANES] = r.astype(BF16)
    y = proj(COL_VB, 256)
    vbf_ref[...] = y
    vbb_ref[...] = y.astype(BF16)
    put_values_t(vbt_ref, y)
    for t, y in enumerate(tiles(proj(COL_QI, 256))):
        r = rope64(y)
        qi_ref[:, t * LANES:(t + 1) * LANES] = r.astype(BF16)
        qit_ref[t * LANES:(t + 1) * LANES, :] = r.T.astype(BF16)
    y = proj(COL_KIWI, LANES)
    is_key = lane < IDX_DIM
    kiwi = jnp.where(is_key, rope64(y), y * (IDX_HEADS ** -0.5))
    kiwi_ref[...] = kiwi
    ki_ref[...] = kiwi[:, 0:IDX_DIM].astype(BF16)
    ki2 = jnp.where(is_key, kiwi, pltpu.roll(kiwi, IDX_DIM, 1)).astype(BF16)
    ki4_ref[:, 0:LANES] = ki2
    ki4_ref[:, LANES:2 * LANES] = ki2
    wt_ref[...] = kiwi.T[IDX_DIM:IDX_DIM + 8, :]


def _project(h, w1, tabs):
    n = h.shape[0]

    def row(width):
        return pl.BlockSpec((TILE_ROWS, width), lambda i: (i, 0))

    def col(height):
        return (pl.BlockSpec((height, TILE_ROWS), lambda i: (0, i)), (height, n))

    def vals(heads):
        return (pl.BlockSpec((heads, None, V_ROWS, TILE_ROWS), lambda i: (0, i, 0, 0)),
                (heads, n // TILE_ROWS, V_ROWS, TILE_ROWS))

    def rows(width):
        return (row(width), (n, width))

    outs = [rows(1024) + (BF16,), col(1024) + (BF16,), rows(1024) + (F32,), rows(1024) + (BF16,),
            rows(1024) + (F32,), rows(1024) + (BF16,), vals(A_HEADS) + (BF16,),
            rows(1024) + (BF16,), col(1024) + (BF16,), rows(256) + (F32,), rows(256) + (BF16,),
            rows(256) + (F32,), rows(256) + (BF16,), vals(B_KV_HEADS) + (BF16,),
            rows(256) + (BF16,), col(256) + (BF16,), rows(LANES) + (F32,), rows(IDX_DIM) + (BF16,),
            rows(256) + (BF16,), col(8) + (F32,)]
    return pl.pallas_call(
        _proj_kernel, grid=(n // TILE_ROWS,),
        in_specs=[row(D_MODEL), pl.BlockSpec((D_MODEL, N_PROJ), lambda i: (0, 0))] + [row(LANES)] * 4,
        out_specs=[spec for spec, _, _ in outs],
        out_shape=[jax.ShapeDtypeStruct(shape, dt) for _, shape, dt in outs],
        compiler_params=_params("parallel"), name="in_proj")(h, w1, *tabs)


def _subln(o1, o2, lam, g, lam_init):
    o = o1 - lam * o2
    y = o * lax.rsqrt(jnp.mean(o * o, -1, keepdims=True) + LN_EPS) * g
    return y * (1.0 - lam_init)


def _diff_p_kernel(lam_ref, qt_ref, k_ref, vt_ref, g_ref, o_ref, qq_sc, ta_sc, tb_sc, m_sc, acc_sc, *,
                   lam_init):
    tq = qt_ref.shape[1]
    i = pl.program_id(1)
    qt = qt_ref[...]
    row = lax.broadcasted_iota(I32, (LANES, 1), 0)
    zero = jnp.zeros_like(qt)
    qq_sc[...] = jnp.concatenate([jnp.where(row < A_HEAD_DIM, qt, zero),
                                  jnp.where(row >= A_HEAD_DIM, qt, zero)], axis=1)
    q_pos = i * tq + lax.broadcasted_iota(I32, (1, tq), 1)
    q_chunk = jnp.concatenate([q_pos, q_pos], axis=1) >> 6
    m_sc[...] = jnp.full(m_sc.shape, NEG_INF, F32)
    acc_sc[...] = jnp.zeros(acc_sc.shape, F32)
    n_kv = ((i + 1) * tq + TILE_K - 1) // TILE_K

    def scores(j):
        return _dot(k_ref[_tile_rows(j), :], qq_sc[...])

    def update(t_ref, j):
        _flash_update_t(t_ref, m_sc, acc_sc, [vt_ref[VT_PER_K * j + c] for c in range(VT_PER_K)], tq)

    def masked_scores(j):
        return jnp.where(_chunk_mask_t(j, q_chunk), scores(j), NEG_INF)

    last = n_kv - 1
    ta_sc[...] = masked_scores(0)

    @pl.when(n_kv == 1)
    def _():
        update(ta_sc, 0)

    def pair(j):
        tb_sc[...] = scores(j + 1)
        update(ta_sc, j)
        ta_sc[...] = scores(j + 2)
        update(tb_sc, j + 1)

    n_octs = jnp.maximum(n_kv - 2, 0) // 8

    def oct_(r, carry):
        for u in range(4):
            pair(8 * r + 2 * u)
        return carry

    lax.fori_loop(0, n_octs, oct_, 0)
    n_quads = jnp.maximum(n_kv - 2, 0) // 4

    def quad(r, carry):
        pair(4 * r)
        pair(4 * r + 2)
        return carry

    lax.fori_loop(2 * n_octs, n_quads, quad, 0)
    extra_pair = (n_kv > 1) & (last - 4 * n_quads >= 3)

    @pl.when(extra_pair)
    def _():
        pair(4 * n_quads)

    j_cur = 4 * n_quads + 2 * extra_pair.astype(I32)

    @pl.when((n_kv > 1) & (last - j_cur == 1))
    def _():
        tb_sc[...] = masked_scores(last)
        update(ta_sc, j_cur)
        update(tb_sc, last)

    @pl.when((n_kv > 1) & (last - j_cur == 2))
    def _():
        tb_sc[...] = scores(j_cur + 1)
        update(ta_sc, j_cur)
        ta_sc[...] = masked_scores(last)
        update(tb_sc, j_cur + 1)
        update(ta_sc, last)

    acc = acc_sc[...]
    outs = [acc[0:LANES, mp * tq:(mp + 1) * tq] / acc[LANES:LANES + 1, mp * tq:(mp + 1) * tq] for mp in range(2)]
    o = outs[0] - lam_ref[0] * outs[1]
    y = o * lax.rsqrt(jnp.mean(o * o, axis=0, keepdims=True) + LN_EPS) * g_ref[...] * (1.0 - lam_init)
    o_ref[...] = y.T.astype(BF16)


def _diff_attn_prompt(lam, qat, ka, vat, g_col, lp, lam_init):
    n = ka.shape[0]
    return pl.pallas_call(
        functools.partial(_diff_p_kernel, lam_init=lam_init),
        grid=(A_HEADS, lp // TILE_Q_A),
        in_specs=[pl.BlockSpec(memory_space=pltpu.SMEM),
                  pl.BlockSpec((LANES, TILE_Q_A), lambda h, i: (h, i)),
                  pl.BlockSpec((lp, LANES), lambda h, i: (0, h)),
                  pl.BlockSpec((None, lp // TILE_ROWS, V_ROWS, TILE_ROWS), lambda h, i: (h, 0, 0, 0)),
                  pl.BlockSpec((LANES, 1), lambda h, i: (0, 0))],
        out_specs=pl.BlockSpec((TILE_Q_A, LANES), lambda h, i: (i, h)),
        out_shape=jax.ShapeDtypeStruct((n, A_HEADS * LANES), BF16),
        scratch_shapes=[pltpu.VMEM((LANES, 2 * TILE_Q_A), BF16),
                        pltpu.VMEM((TILE_K, 2 * TILE_Q_A), F32), pltpu.VMEM((TILE_K, 2 * TILE_Q_A), F32),
                        pltpu.VMEM((1, 2 * TILE_Q_A), F32), pltpu.VMEM((V_ROWS, 2 * TILE_Q_A), F32)],
        compiler_params=_params("parallel", "parallel"), name="diff_attn_prompt")(lam, qat, ka, vat, g_col)


def _diff_s_kernel(lam_ref, q_ref, ck_ref, cv_ref, kn_ref, vn_ref, g_ref, prev_ref, o_ref, *, lam_init, n_keys):
    del prev_ref
    ds_ = q_ref.shape[0]
    past = ck_ref.shape[0]
    pad = n_keys - past - ds_
    lane = lax.broadcasted_iota(I32, (1, LANES), 1)
    valid = lax.broadcasted_iota(I32, (1, n_keys), 1) < past + ds_
    for h in range(A_HEADS):
        cols = slice(h * LANES, (h + 1) * LANES)
        q = q_ref[:, cols]
        zero = jnp.zeros_like(q)
        k = jnp.concatenate([ck_ref[:, cols].astype(BF16), kn_ref[:, cols], jnp.zeros((pad, LANES), BF16)], 0)
        v = jnp.concatenate([cv_ref[:, cols].astype(BF16), vn_ref[:, cols], jnp.zeros((pad, LANES), BF16)], 0)
        qz = jnp.concatenate([jnp.where(lane < A_HEAD_DIM, q, zero), jnp.where(lane >= A_HEAD_DIM, q, zero)], 0)
        s = jnp.where(valid, _dot_nt(qz, k) * (A_HEAD_DIM ** -0.5), NEG_INF)
        p = jnp.exp(s - jnp.max(s, axis=1, keepdims=True))
        o = _dot(p.astype(BF16), v) / jnp.sum(p, axis=1, keepdims=True)
        o_ref[:, cols] = _subln(o[0:ds_], o[ds_:2 * ds_], lam_ref[0], g_ref[...], lam_init).astype(BF16)


def _diff_attn_sample(lam, qa, cache_k, cache_v, ka, va, g, ya, layer, lp, lam_init):
    _, nb, past, width = cache_k.shape
    n = qa.shape[0]
    ds_ = (n - lp) // nb
    n_keys = -(-(past + ds_) // LANES) * LANES
    base = lp // ds_
    new = pl.BlockSpec((ds_, width), lambda b: (base + b, 0))
    cache = pl.BlockSpec((None, None, past, width), lambda b: (layer, b, 0, 0))
    return pl.pallas_call(
        functools.partial(_diff_s_kernel, lam_init=lam_init, n_keys=n_keys), grid=(nb,),
        in_specs=[pl.BlockSpec(memory_space=pltpu.SMEM), new, cache, cache, new, new,
                  pl.BlockSpec((1, LANES), lambda b: (0, 0)), pl.BlockSpec(memory_space=pl.ANY)],
        out_specs=new, out_shape=jax.ShapeDtypeStruct(ya.shape, ya.dtype),
        input_output_aliases={7: 0},
        compiler_params=_params("parallel"), name="diff_attn_sample")(lam, qa, cache_k, cache_v, ka, va, g, ya)


def _index_scores(qiz, w_cols, ki4, rows):
    r = _dot_nt(qiz, ki4)
    sc = w_cols[0] * jnp.maximum(r[0:rows], 0.0)
    for h in range(1, IDX_HEADS):
        sc = sc + w_cols[h] * jnp.maximum(r[h * rows:(h + 1) * rows], 0.0)
    return sc


def _stack_idx_queries(qi, rows):
    lane = lax.broadcasted_iota(I32, (1, IDX_HEADS * IDX_DIM), 1)
    zero = jnp.zeros_like(qi)
    return [jnp.where((lane >= IDX_DIM * h) & (lane < IDX_DIM * (h + 1)), qi, zero) for h in range(IDX_HEADS)]


def _selected(key, t, need, seen, u):
    eq = key == t
    prefix = _dot(jnp.where(eq, 1.0, 0.0).astype(BF16), u)
    sel = ((key > t) | (eq & (seen + prefix <= need))) & (key > KEY_HALF_NEG)
    return sel, seen + prefix[:, -1:]


def _dsa_p_kernel(qbt_ref, qit_ref, wt_ref, kb_ref, vbt_ref, ki_ref, tri_ref, o_ref,
                  key_sc, qq_sc, ta_sc, tb_sc, sel_sc, m_sc, acc_sc, *, n_sel):
    tq = qbt_ref.shape[1]
    i = pl.program_id(0)
    n_kv = ((i + 1) * tq + TILE_K - 1) // TILE_K
    w = wt_ref[...] * (IDX_DIM ** -0.5)
    w_rows = [w[h:h + 1] for h in range(IDX_HEADS)]
    q_chunk = (i * tq + lax.broadcasted_iota(I32, (1, tq), 1)) >> 6

    last = n_kv - 1

    def put_products(t_ref, j):
        ki = ki_ref[_tile_rows(j), :]
        for h in range(IDX_HEADS):
            t_ref[:, h * tq:(h + 1) * tq] = _dot(ki, qit_ref[h * IDX_DIM:(h + 1) * IDX_DIM, :])

    def put_keys(t_ref, j, top):
        sc = None
        for h in range(IDX_HEADS):
            r = w_rows[h] * jnp.maximum(t_ref[:, h * tq:(h + 1) * tq], 0.0)
            sc = r if sc is None else sc + r
        key = _sort_key(jnp.where(_chunk_mask_t(j, q_chunk), sc, NEG_INF))
        key_sc[j] = key
        return jnp.maximum(top, jnp.max(key, axis=0, keepdims=True))

    put_products(ta_sc, 0)

    def score_pair(r, top):
        j = 2 * r
        j1 = jnp.minimum(j + 1, last)
        put_products(tb_sc, j1)
        top = put_keys(ta_sc, j, top)
        put_products(ta_sc, jnp.minimum(j + 2, last))
        return put_keys(tb_sc, j1, top)

    top_key = lax.fori_loop(0, (n_kv + 1) // 2, score_pair, jnp.full((1, tq), INT_MIN, I32))

    def count_ge(c):
        def body(j, acc):
            parts = [None] * COUNT_CHAINS
            for n, r in enumerate(range(0, TILE_K, COUNT_ROWS)):
                hit = jnp.where(key_sc[j, r:r + COUNT_ROWS, :] >= c, 1.0, 0.0)
                parts[n % COUNT_CHAINS] = hit if parts[n % COUNT_CHAINS] is None else parts[n % COUNT_CHAINS] + hit
            return acc + functools.reduce(lambda a, b: a + b, parts)

        def two(r, acc):
            return body(2 * r + 1, body(2 * r, acc))

        acc = lax.fori_loop(0, n_kv // 2, two, jnp.zeros((COUNT_ROWS, tq), F32))
        acc = lax.fori_loop(2 * (n_kv // 2), n_kv, body, acc)
        return jnp.sum(acc, axis=0, keepdims=True)

    thr = _kth_largest_key_bracketed(count_ge, top_key, float(n_sel))
    need = float(n_sel) - count_ge(thr + 1)
    need = jnp.where(thr > KEY_HALF_NEG, need, 0.0)
    floor_key = jnp.maximum(thr, KEY_HALF_NEG)

    for c in range(B_KV_HEADS):
        for g in range(B_GROUP):
            hh = c * B_GROUP + g
            qq_sc[c, :, g * tq:(g + 1) * tq] = qbt_ref[hh * LANES:(hh + 1) * LANES, :]
    m_sc[...] = jnp.full(m_sc.shape, NEG_INF, F32)
    acc_sc[...] = jnp.zeros(acc_sc.shape, F32)

    def put_scores(t_ref, j, c):
        s = _dot(kb_ref[_tile_rows(j), c * LANES:(c + 1) * LANES], qq_sc[c])
        sel = sel_sc[...] > 0.5
        for g in range(B_GROUP):
            t_ref[:, g * tq:(g + 1) * tq] = jnp.where(sel, s[:, g * tq:(g + 1) * tq], NEG_INF)

    def update(t_ref, j, c):
        _flash_update_t(t_ref, m_sc.at[c], acc_sc.at[c],
                        [vbt_ref[c, VT_PER_K * j + cc] for cc in range(VT_PER_K)], tq)

    def select(j, seen):
        key = key_sc[j]
        tie = key == thr
        half = TILE_K // 2
        ones = jnp.where(tie, 1.0, 0.0).astype(BF16)
        for r in range(2):
            rows = slice(r * half, (r + 1) * half)
            prefix = seen + _dot(tri_ref[...], ones[rows])
            sel_sc[rows, :] = jnp.where((key[rows] > floor_key) | (tie[rows] & (prefix <= need)), 1.0, 0.0)
            seen = prefix[half - 1:half, :]
        return seen

    seen0 = select(0, jnp.zeros((1, tq), F32))
    put_scores(ta_sc, 0, 0)

    def attend(j, seen):
        nxt = jnp.minimum(j + 1, last)
        put_scores(tb_sc, j, 1)
        update(ta_sc, j, 0)
        seen = select(nxt, seen)
        put_scores(ta_sc, nxt, 0)
        update(tb_sc, j, 1)
        return seen

    def attend_four(r, seen):
        for u in range(4):
            seen = attend(4 * r + u, seen)
        return seen

    seen_mid = lax.fori_loop(0, n_kv // 4, attend_four, seen0)
    lax.fori_loop(4 * (n_kv // 4), n_kv, attend, seen_mid)
    for c in range(B_KV_HEADS):
        acc = acc_sc[c]
        for g in range(B_GROUP):
            hh = c * B_GROUP + g
            a = acc[:, g * tq:(g + 1) * tq]
            o_ref[:, hh * LANES:(hh + 1) * LANES] = (a[0:LANES] / a[LANES:LANES + 1]).T.astype(BF16)


def _dsa_prompt(qbt, qit, wt, kb, vbt, ki, tri, lp, n_sel):
    n = kb.shape[0]
    tq = TILE_Q_B

    def col(height):
        return pl.BlockSpec((height, tq), lambda i: (0, i))

    def resident(shape):
        return pl.BlockSpec(shape, lambda i: (0,) * len(shape), pipeline_mode=pl.Buffered(1))

    return pl.pallas_call(
        functools.partial(_dsa_p_kernel, n_sel=n_sel), grid=(lp // tq,),
        in_specs=[col(1024), col(256), col(8), resident((lp, 256)),
                  resident((B_KV_HEADS, lp // TILE_ROWS, V_ROWS, TILE_ROWS)), resident((lp, IDX_DIM)),
                  resident((TILE_K // 2, TILE_K // 2))],
        out_specs=pl.BlockSpec((tq, 1024), lambda i: (i, 0)),
        out_shape=jax.ShapeDtypeStruct((n, 1024), BF16),
        scratch_shapes=[pltpu.VMEM((lp // TILE_K, TILE_K, tq), I32),
                        pltpu.VMEM((B_KV_HEADS, LANES, B_GROUP * tq), BF16),
                        pltpu.VMEM((TILE_K, B_GROUP * tq), F32), pltpu.VMEM((TILE_K, B_GROUP * tq), F32),
                        pltpu.VMEM((TILE_K, tq), F32),
                        pltpu.VMEM((B_KV_HEADS, 1, B_GROUP * tq), F32),
                        pltpu.VMEM((B_KV_HEADS, V_ROWS, B_GROUP * tq), F32)],
        compiler_params=_params("parallel"), name="dsa_prompt")(qbt, qit, wt, kb, vbt, ki, tri)


def _dsa_s_kernel(qb_ref, qi_ref, kiwi_ref, ckb_ref, cvb_ref, cki_ref, kbn_ref, vbn_ref, ki4n_ref,
                  rep_ref, u_ref, prev_ref, o_ref, *, n_sel, n_keys):
    del prev_ref
    tq = qb_ref.shape[0]
    past = ckb_ref.shape[0]
    pad = n_keys - past - tq
    width = B_KV_HEADS * B_HEAD_DIM
    kb = jnp.concatenate([ckb_ref[...].astype(BF16), kbn_ref[...], jnp.zeros((pad, width), BF16)], 0)
    vb = jnp.concatenate([cvb_ref[...].astype(BF16), vbn_ref[...], jnp.zeros((pad, width), BF16)], 0)
    ki4_past = _dot(cki_ref[...].astype(BF16), rep_ref[...]).astype(BF16)
    ki4 = jnp.concatenate([ki4_past, ki4n_ref[...], jnp.zeros((pad, width), BF16)], 0)
    qiz = jnp.concatenate(_stack_idx_queries(qi_ref[...], tq), 0)
    wq = kiwi_ref[...] * (IDX_DIM ** -0.5)
    w_cols = [wq[:, IDX_DIM + h:IDX_DIM + h + 1] for h in range(IDX_HEADS)]
    sc = _index_scores(qiz, w_cols, ki4, tq)
    valid = lax.broadcasted_iota(I32, (1, n_keys), 1) < past + tq
    key = _sort_key(jnp.where(valid, sc, NEG_INF))

    def count_ge(c):
        return jnp.sum(jnp.where(key >= c, 1.0, 0.0), axis=1, keepdims=True)

    t = _kth_largest_key(count_ge, (tq, 1), float(n_sel))
    need = float(n_sel) - count_ge(t + 1)
    sel, _ = _selected(key, t, need, jnp.zeros((tq, 1), F32), u_ref[...])
    for c in range(B_KV_HEADS):
        cols = slice(c * LANES, (c + 1) * LANES)
        q = jnp.concatenate([qb_ref[:, (c * B_GROUP + g) * LANES:(c * B_GROUP + g + 1) * LANES]
                             for g in range(B_GROUP)], 0)
        s_all = _dot_nt(q, kb[:, cols]) * (B_HEAD_DIM ** -0.5)
        for g in range(B_GROUP):
            hh = c * B_GROUP + g
            s = jnp.where(sel, s_all[g * tq:(g + 1) * tq], NEG_INF)
            p = jnp.exp(s - jnp.max(s, axis=1, keepdims=True))
            o = _dot(p.astype(BF16), vb[:, cols]) / jnp.sum(p, axis=1, keepdims=True)
            o_ref[:, hh * LANES:(hh + 1) * LANES] = o.astype(BF16)


def _dsa_sample(qb, qi, kiwi, cache_k, cache_v, cache_idx, kb, vb, ki4, rep, u, yb, layer, lp, n_sel):
    _, nb, past, _ = cache_k.shape
    n = qb.shape[0]
    ds_ = (n - lp) // nb
    n_keys = u.shape[0]
    base = lp // ds_

    def new(width):
        return pl.BlockSpec((ds_, width), lambda b: (base + b, 0))

    def cache(width):
        return pl.BlockSpec((None, None, past, width), lambda b: (layer, b, 0, 0))

    def const(shape):
        return pl.BlockSpec(shape, lambda b: (0, 0))

    return pl.pallas_call(
        functools.partial(_dsa_s_kernel, n_sel=n_sel, n_keys=n_keys), grid=(nb,),
        in_specs=[new(1024), new(256), new(LANES), cache(256), cache(256), cache(IDX_DIM),
                  new(256), new(256), new(256), const(rep.shape), const(u.shape),
                  pl.BlockSpec(memory_space=pl.ANY)],
        out_specs=new(1024), out_shape=jax.ShapeDtypeStruct(yb.shape, yb.dtype),
        input_output_aliases={11: 0},
        compiler_params=_params("parallel"), name="dsa_sample")(
            qb, qi, kiwi, cache_k, cache_v, cache_idx, kb, vb, ki4, rep, u, yb)


def _merge_kernel(h_ref, ya_ref, yb_ref, wga_ref, wgb_ref, wpa_ref, wpb_ref, wo_ref, g_ref, b_ref,
                  wr_ref, bias_ref, o_ref, gates_ref):
    h = h_ref[...]
    hb = h.astype(BF16)
    mixed = (jax.nn.sigmoid(_dot(hb, wga_ref[...])) * _dot(ya_ref[...], wpa_ref[...])
             + jax.nn.sigmoid(_dot(hb, wgb_ref[...])) * _dot(yb_ref[...], wpb_ref[...]))
    y = _dot(mixed.astype(BF16), wo_ref[...])
    h1 = _layernorm(DEEPNORM_ALPHA * h + y, g_ref[...], b_ref[...])
    o_ref[...] = h1
    gates_ref[...] = _route_t(jax.nn.sigmoid(_dot_nt(wr_ref[...], h1.astype(BF16))), bias_ref[...])


def _merge(h, ya, yb, wga, wgb, wpa, wpb, wo, g, b, wr_t, bias):
    n, d = h.shape
    row = pl.BlockSpec((TILE_ROWS, d), lambda i: (i, 0))
    mat = pl.BlockSpec((d, d), lambda i: (0, 0))
    vec = pl.BlockSpec((1, d), lambda i: (0, 0))
    return pl.pallas_call(
        _merge_kernel, grid=(n // TILE_ROWS,),
        in_specs=[row, row, row] + [mat] * 5 + [vec, vec, pl.BlockSpec((N_EXPERTS, d), lambda i: (0, 0)),
                                                 pl.BlockSpec((N_EXPERTS, 1), lambda i: (0, 0))],
        out_specs=[row, pl.BlockSpec((N_EXPERTS, TILE_ROWS), lambda i: (0, i))],
        out_shape=[jax.ShapeDtypeStruct((n, d), F32), jax.ShapeDtypeStruct((N_EXPERTS, n), F32)],
        compiler_params=_params("parallel"), name="merge_ln1_route")(
            h, ya, yb, wga, wgb, wpa, wpb, wo, g.reshape(1, d), b.reshape(1, d), wr_t,
            bias.reshape(N_EXPERTS, 1))


def _route_t(s, bias):
    sel = s + bias
    row_f = lax.broadcasted_iota(I32, (N_EXPERTS, 1), 0).astype(F32)
    neg = -jnp.inf

    def first_argmax(x, m, rows):
        return jnp.min(jnp.where(x == m, rows, float(N_EXPERTS)), axis=0, keepdims=True)

    groups = [slice(g * GROUP_SIZE, (g + 1) * GROUP_SIZE) for g in range(N_GROUPS)]
    g_score = []
    for rows in groups:
        x = sel[rows]
        m1 = jnp.max(x, axis=0, keepdims=True)
        x = jnp.where(row_f[rows] == first_argmax(x, m1, row_f[rows]), neg, x)
        g_score.append(m1 + jnp.max(x, axis=0, keepdims=True))
    kept = []
    for g, rows in enumerate(groups):
        ahead = jnp.zeros_like(g_score[g])
        for o in range(N_GROUPS):
            if o != g:
                beats = (g_score[o] > g_score[g]) | ((g_score[o] == g_score[g]) & (o < g))
                ahead = ahead + jnp.where(beats, 1.0, 0.0)
        kept.append(jnp.where(ahead < float(TOPK_GROUPS), sel[rows], NEG_INF))
    x = jnp.concatenate(kept, axis=0)
    chosen = None
    for _ in range(TOP_K):
        hit = row_f == first_argmax(x, jnp.max(x, axis=0, keepdims=True), row_f)
        chosen = hit if chosen is None else (chosen | hit)
        x = jnp.where(hit, neg, x)
    w = jnp.where(chosen, s, 0.0)
    return w / jnp.sum(w, axis=0, keepdims=True) * ROUTE_SCALE


def _silu(x):
    return x * jax.nn.sigmoid(x)


def _moe_kernel(h_ref, gates_ref, wg_ref, wu_ref, wd_ref, sg_ref, su_ref, sd_ref, g_ref, b_ref, o_ref,
                xb_sc, acc_sc):
    step = pl.program_id(1)

    @pl.when(step == 0)
    def _():
        xb = h_ref[...].astype(BF16)
        xb_sc[...] = xb
        a = _silu(_dot(xb, sg_ref[...])) * _dot(xb, su_ref[...])
        acc_sc[...] = _dot(a.astype(BF16), sd_ref[...])

    xb = xb_sc[...]
    g_rows = gates_ref[...]
    gates = jnp.concatenate([g_rows, jnp.zeros((LANES - EXPERT_BLOCK, g_rows.shape[1]), F32)], axis=0).T
    acts = []
    for e in range(EXPERT_BLOCK):
        a = _silu(_dot(xb, wg_ref[e])) * _dot(xb, wu_ref[e]) * gates[:, e:e + 1]
        acts.append(a.astype(BF16))
    wd = wd_ref[...]
    acc_sc[...] += _dot(jnp.concatenate(acts, axis=1), wd.reshape(wd.shape[0] * wd.shape[1], wd.shape[2]))

    @pl.when(step == pl.num_programs(1) - 1)
    def _():
        o_ref[...] = _layernorm(DEEPNORM_ALPHA * h_ref[...] + acc_sc[...], g_ref[...], b_ref[...])


def _moe(h, gates, wg, wu, wd, sg, su, sd, g, b):
    n, d = h.shape
    f = wg.shape[-1]
    row = pl.BlockSpec((TILE_MOE, d), lambda r, e: (r, 0))
    vec = pl.BlockSpec((1, d), lambda r, e: (0, 0))
    return pl.pallas_call(
        _moe_kernel, grid=(n // TILE_MOE, N_EXPERTS // EXPERT_BLOCK),
        in_specs=[row, pl.BlockSpec((EXPERT_BLOCK, TILE_MOE), lambda r, e: (e, r)),
                  pl.BlockSpec((EXPERT_BLOCK, d, f), lambda r, e: (e, 0, 0)),
                  pl.BlockSpec((EXPERT_BLOCK, d, f), lambda r, e: (e, 0, 0)),
                  pl.BlockSpec((EXPERT_BLOCK, f, d), lambda r, e: (e, 0, 0)),
                  pl.BlockSpec((d, sg.shape[1]), lambda r, e: (0, 0)),
                  pl.BlockSpec((d, sg.shape[1]), lambda r, e: (0, 0)),
                  pl.BlockSpec((sg.shape[1], d), lambda r, e: (0, 0)), vec, vec],
        out_specs=row, out_shape=jax.ShapeDtypeStruct((n, d), F32),
        scratch_shapes=[pltpu.VMEM((TILE_MOE, d), BF16), pltpu.VMEM((TILE_MOE, d), F32)],
        compiler_params=_params("parallel", "arbitrary"), name="moe_ln2")(
            h, gates, wg, wu, wd, sg, su, sd, g.reshape(1, d), b.reshape(1, d))


def _rope_tables(pos):
    def table(half, reps):
        inv = ROPE_THETA ** (-jnp.arange(half, dtype=F32) / half)
        ang = pos.astype(F32)[:, None] * inv[None, :]
        cos, sin = jnp.cos(ang), jnp.sin(ang)
        return (jnp.tile(jnp.concatenate([cos, cos], -1), (1, reps)),
                jnp.tile(jnp.concatenate([-sin, sin], -1), (1, reps)))
    c64, s64 = table(A_HEAD_DIM // 2, 2)
    c128, s128 = table(B_HEAD_DIM // 2, 1)
    return c64, s64, c128, s128


def kernel(x_prompt, x_sample, cache_a_k, cache_a_v, cache_b_k, cache_b_v, cache_b_idx, meta_tokens, ln_in_g, ln_in_b, w_in, lam_q1, lam_k1, lam_q2, lam_k2, subln_g, w_proj_a, w_proj_b, w_out, ln1_g, ln1_b, w_router, router_bias, w_expert_gate, w_expert_up, w_expert_down, w_shared_gate, w_shared_up, w_shared_down, ln2_g, ln2_b):
    batch, seq, d = x_prompt.shape
    nb, ds_, _ = x_sample.shape
    depth, _, past = cache_a_k.shape[:3]
    assert batch == 1 and d == D_MODEL and seq % TILE_K == 0 and ds_ == CHUNK
    lp = FRONT_PAD + N_META + seq
    ns = nb * ds_
    n = lp + ns
    assert n % TILE_MOE == 0 and past % 16 == 0
    n_sel_p = min(TOPK_MAX, seq // 4)
    n_sel_s = min(TOPK_MAX, (past + ds_) // 4)
    n_keys_s = -(-(past + ds_) // LANES) * LANES

    x_all = jnp.concatenate([jnp.zeros((FRONT_PAD, d), F32), meta_tokens.astype(F32), x_prompt[0],
                             x_sample.reshape(ns, d)], 0)
    pos = jnp.concatenate([jnp.maximum(jnp.arange(lp, dtype=jnp.int32) - FRONT_PAD, 0),
                           jnp.tile(past + jnp.arange(ds_, dtype=jnp.int32), nb)])
    tabs = _rope_tables(pos)
    tri_p = jnp.tril(jnp.ones((TILE_K // 2, TILE_K // 2), BF16))
    u_s = jnp.triu(jnp.ones((n_keys_s, n_keys_s), BF16))
    rep = jnp.tile(jnp.eye(IDX_DIM, dtype=BF16), (1, IDX_HEADS))
    ck_a = cache_a_k.reshape(depth, nb, past, -1)
    cv_a = cache_a_v.reshape(depth, nb, past, -1)
    ck_b = cache_b_k.reshape(depth, nb, past, -1)
    cv_b = cache_b_v.reshape(depth, nb, past, -1)

    h = _ln_rows(x_all, ln_in_g, ln_in_b)
    rows = []
    for l in range(depth):
        lam_init = 0.8 - 0.6 * math.exp(-0.3 * l)
        lam = (jnp.exp(jnp.sum(lam_q1[l].astype(F32) * lam_k1[l].astype(F32)))
               - jnp.exp(jnp.sum(lam_q2[l].astype(F32) * lam_k2[l].astype(F32))) + lam_init).reshape(1)
        w1 = jnp.concatenate([w_in[l][:, :N_RAW_PROJ], jnp.zeros((d, N_PROJ - N_RAW_PROJ), F32)], 1).astype(BF16)
        wga = w_in[l][:, N_RAW_PROJ:N_RAW_PROJ + d].astype(BF16)
        wgb = w_in[l][:, N_RAW_PROJ + d:].astype(BF16)
        (qa, qat, ka_f, ka, va_f, va, vat, qb, qbt, kb_f, kb, vb_f, vb, vbt, qi, qit, kiwi, ki, ki4,
         wt) = _project(h, w1, tabs)
        rows.append((ka_f, va_f, kb_f, vb_f, kiwi))
        g = subln_g[l].astype(F32)
        ya = _diff_attn_prompt(lam, qat, ka, vat, g.reshape(LANES, 1), lp, lam_init)
        ya = _diff_attn_sample(lam, qa, ck_a, cv_a, ka, va, g.reshape(1, LANES), ya, l, lp, lam_init)
        yb = _dsa_prompt(qbt, qit, wt, kb, vbt, ki, tri_p, lp, n_sel_p)
        yb = _dsa_sample(qb, qi, kiwi, ck_b, cv_b, cache_b_idx, kb, vb, ki4, rep, u_s, yb, l, lp, n_sel_s)
        h, gates = _merge(h, ya, yb, wga, wgb, w_proj_a[l].astype(BF16), w_proj_b[l].astype(BF16),
                          w_out[l].astype(BF16), ln1_g[l], ln1_b[l], w_router[l].T.astype(BF16),
                          router_bias[l].astype(F32))
        h = _moe(h, gates, w_expert_gate[l].astype(BF16), w_expert_up[l].astype(BF16),
                 w_expert_down[l].astype(BF16), w_shared_gate[l].astype(BF16), w_shared_up[l].astype(BF16),
                 w_shared_down[l].astype(BF16), ln2_g[l], ln2_b[l])

    p0 = FRONT_PAD
    y_prompt = h[p0 + N_META:lp].reshape(1, seq, d)
    y_sample = h[lp:].reshape(nb, ds_, d)

    def stack(idx, width, tail):
        return (jnp.stack([r[idx][p0:lp, :width] for r in rows]).reshape((depth, 1, lp - p0) + tail),
                jnp.stack([r[idx][lp:, :width] for r in rows]).reshape((depth, nb, ds_) + tail))

    ak_p, ak_s = stack(0, 1024, (2 * A_HEADS, A_HEAD_DIM))
    av_p, av_s = stack(1, 1024, (A_HEADS, 2 * A_HEAD_DIM))
    bk_p, bk_s = stack(2, 256, (B_KV_HEADS, B_HEAD_DIM))
    bv_p, bv_s = stack(3, 256, (B_KV_HEADS, B_HEAD_DIM))
    bi_p, bi_s = stack(4, IDX_DIM, (IDX_DIM,))
    return (y_prompt, y_sample, ak_p, av_p, bk_p, bv_p, bi_p, ak_s, av_s, bk_s, bv_s, bi_s)
```
